```python
import jax, jax.numpy as jnp
from jax import lax
import numpy as np

D_MODEL = 2048
BATCH = 4
SEQ = 2048
DEPTH = 1
DEC_BATCH = 128
DEC_SEQ = 4
PAST_LEN = 16384
PAGE_SIZE = 128

CHUNK = 128
HEAD_DIM_A = 128
D_A = D_MODEL
H_A = D_A // HEAD_DIM_A
D_B = D_MODEL
K_CONV = 31
D_FF = 5632
EPS = 1e-6
SPLITS = (D_A, 2 * D_A, 2 * D_A + D_B, 2 * D_A + 2 * D_B, 2 * D_A + 2 * D_B + D_MODEL)
D_IN = 2 * D_A + 2 * D_B + 2 * D_MODEL

kernel_name = "hybrid_gmlp_conformer_conv_decode_step"


def rms_norm(x, g):
    xf = x.astype(jnp.float32)
    y = xf * lax.rsqrt(jnp.mean(xf * xf, axis=-1, keepdims=True) + EPS)
    return (y * g.astype(jnp.float32)).astype(x.dtype)


def layer_norm(x, g, b):
    xf = x.astype(jnp.float32)
    mu = jnp.mean(xf, axis=-1, keepdims=True)
    var = jnp.mean(jnp.square(xf - mu), axis=-1, keepdims=True)
    y = (xf - mu) * lax.rsqrt(var + EPS)
    return (y * g.astype(jnp.float32) + b.astype(jnp.float32)).astype(x.dtype)


def swiglu_ffn(x, w_gu, w_down):
    gate, up = jnp.split(x @ w_gu, 2, axis=-1)
    return (jax.nn.silu(gate) * up) @ w_down


def chunk_spatial_gating(u, v, w_s, b_s):
    n, t, _ = v.shape
    n_chunks = -(-t // CHUNK)
    pad = n_chunks * CHUNK - t
    vp = jnp.pad(v, ((0, 0), (0, pad), (0, 0)))
    vc = vp.reshape(n, n_chunks, CHUNK, H_A, HEAD_DIM_A)
    mask = jnp.tril(jnp.ones((CHUNK, CHUNK), dtype=w_s.dtype))
    w = w_s * mask[None]
    s = jnp.einsum('hij,ncjhd->ncihd', w, vc) + b_s.T[None, None, :, :, None]
    s = s.reshape(n, n_chunks * CHUNK, D_A)[:, :t]
    return u * s


def token_mixer(h, conv_prev, w_in, w_s, b_s, ln_v_g, ln_v_b, w_pa,
                w_dw, b_dw, ln_c_g, ln_c_b, w_pb, w_out):
    t = h.shape[1]
    p = h @ w_in
    ua, va, glu_a, glu_b, gate_a, gate_b = jnp.split(p, SPLITS, axis=-1)
    u = jax.nn.gelu(ua)
    v = layer_norm(jax.nn.gelu(va), ln_v_g, ln_v_b)
    y_a = chunk_spatial_gating(u, v, w_s, b_s) @ w_pa
    last_len = t - CHUNK * ((t - 1) // CHUNK)
    v_state = v[:, t - last_len:]
    g = glu_a * jax.nn.sigmoid(glu_b)
    gpad = jnp.concatenate([conv_prev.astype(g.dtype), g], axis=1)
    c = lax.conv_general_dilated(gpad, w_dw[:, None, :].astype(g.dtype), (1,), 'VALID',
                                 dimension_numbers=('NWC', 'WIO', 'NWC'),
                                 feature_group_count=D_B) + b_dw
    c = jax.nn.silu(layer_norm(c, ln_c_g, ln_c_b))
    y_b = c @ w_pb
    conv_state = gpad[:, -(K_CONV - 1):]
    m = jax.nn.sigmoid(gate_a) * y_a + jax.nn.sigmoid(gate_b) * y_b
    return m @ w_out, v_state, conv_state


def decoder_layer(x, conv_prev, ffn1_norm, ffn1_w_gu, ffn1_w_down, mix_norm, w_in, w_s, b_s,
                  ln_v_g, ln_v_b, w_pa, w_dw, b_dw, ln_c_g, ln_c_b, w_pb, w_out,
                  ffn2_norm, ffn2_w_gu, ffn2_w_down):
    x = x + 0.5 * swiglu_ffn(rms_norm(x, ffn1_norm), ffn1_w_gu, ffn1_w_down)
    mix, v_state, conv_state = token_mixer(rms_norm(x, mix_norm), conv_prev, w_in, w_s, b_s,
                                           ln_v_g, ln_v_b, w_pa, w_dw, b_dw, ln_c_g, ln_c_b,
                                           w_pb, w_out)
    x = x + mix
    x = x + 0.5 * swiglu_ffn(rms_norm(x, ffn2_norm), ffn2_w_gu, ffn2_w_down)
    return x, v_state, conv_state


def setup_inputs(seed: int = 0) -> dict:
    key = jax.random.key(seed)
    ks = jax.random.split(key, 24)
    f32 = jnp.float32

    def nrm(k, shape, scale):
        return jax.random.normal(k, shape, f32) * scale

    def gain(k, shape):
        return 1.0 + 0.02 * jax.random.normal(k, shape, f32)

    L = DEPTH
    return {
        "x_prompt": nrm(ks[0], (BATCH, SEQ, D_MODEL), 1.0),
        "x_sample": nrm(ks[1], (DEC_BATCH, DEC_SEQ, D_MODEL), 1.0),
        "cache_conv": nrm(ks[2], (L, DEC_BATCH, K_CONV - 1, D_B), 0.5),
        "ffn1_norm": gain(ks[3], (L, D_MODEL)),
        "ffn1_w_gu": nrm(ks[4], (L, D_MODEL, 2 * D_FF), D_MODEL ** -0.5),
        "ffn1_w_down": nrm(ks[5], (L, D_FF, D_MODEL), D_FF ** -0.5),
        "mix_norm": gain(ks[6], (L, D_MODEL)),
        "w_in": nrm(ks[7], (L, D_MODEL, D_IN), D_MODEL ** -0.5),
        "w_s": nrm(ks[8], (L, H_A, CHUNK, CHUNK), CHUNK ** -0.5),
        "b_s": gain(ks[9], (L, H_A, CHUNK)),
        "ln_v_g": gain(ks[10], (L, D_A)),
        "ln_v_b": nrm(ks[11], (L, D_A), 0.02),
        "w_pa": nrm(ks[12], (L, D_A, D_MODEL), D_A ** -0.5),
        "w_dw": nrm(ks[13], (L, K_CONV, D_B), K_CONV ** -0.5),
        "b_dw": nrm(ks[14], (L, D_B), 0.02),
        "ln_c_g": gain(ks[15], (L, D_B)),
        "ln_c_b": nrm(ks[16], (L, D_B), 0.02),
        "w_pb": nrm(ks[17], (L, D_B, D_MODEL), D_B ** -0.5),
        "w_out": nrm(ks[18], (L, D_MODEL, D_MODEL), D_MODEL ** -0.5),
        "ffn2_norm": gain(ks[19], (L, D_MODEL)),
        "ffn2_w_gu": nrm(ks[20], (L, D_MODEL, 2 * D_FF), D_MODEL ** -0.5),
        "ffn2_w_down": nrm(ks[21], (L, D_FF, D_MODEL), D_FF ** -0.5),
        "final_norm": gain(ks[22], (D_MODEL,)),
    }


def reference(x_prompt, x_sample, cache_conv, ffn1_norm, ffn1_w_gu, ffn1_w_down, mix_norm,
              w_in, w_s, b_s, ln_v_g, ln_v_b, w_pa, w_dw, b_dw, ln_c_g, ln_c_b, w_pb, w_out,
              ffn2_norm, ffn2_w_gu, ffn2_w_down, final_norm):
    xp, xs = x_prompt, x_sample
    vp_list, vs_list, cp_list, cs_list = [], [], [], []
    for l in range(DEPTH):
        params = (ffn1_norm[l], ffn1_w_gu[l], ffn1_w_down[l], mix_norm[l], w_in[l], w_s[l], b_s[l],
                  ln_v_g[l], ln_v_b[l], w_pa[l], w_dw[l], b_dw[l], ln_c_g[l], ln_c_b[l], w_pb[l],
                  w_out[l], ffn2_norm[l], ffn2_w_gu[l], ffn2_w_down[l])
        zero_conv = jnp.zeros((xp.shape[0], K_CONV - 1, D_B), dtype=xp.dtype)
        xp, v_p, c_p = decoder_layer(xp, zero_conv, *params)
        xs, v_s, c_s = decoder_layer(xs, cache_conv[l], *params)
        vp_list.append(v_p)
        vs_list.append(v_s)
        cp_list.append(c_p)
        cs_list.append(c_s)
    y_prompt = rms_norm(xp, final_norm)
    y_sample = rms_norm(xs, final_norm)
    chunk_v_prompt = jnp.stack(vp_list)
    chunk_v_sample = jnp.stack(vs_list)
    conv_state_prompt = jnp.stack(cp_list)
    conv_state_sample = jnp.stack(cs_list)
    return (y_prompt, y_sample, chunk_v_prompt, chunk_v_sample, conv_state_prompt, conv_state_sample)
```

```python
import functools

import jax
import jax.numpy as jnp
from jax import lax
from jax.experimental import pallas as pl
from jax.experimental.pallas import tpu as pltpu

F32 = jnp.float32
BF16 = jnp.bfloat16
EPS = 1e-6

V7X_VMEM_BYTES = 64 * 1024 * 1024
VMEM_LIMIT_BYTES = V7X_VMEM_BYTES - 4 * 1024 * 1024
SUBLANES = 8
CARRY_ROWS = 32


def _params():
    return pltpu.CompilerParams(
        dimension_semantics=("arbitrary", "arbitrary"),
        vmem_limit_bytes=VMEM_LIMIT_BYTES,
    )


def _rms(x, gain):
    ms = jnp.mean(x * x, axis=-1, keepdims=True)
    return x * lax.rsqrt(ms + EPS) * gain


def _layer_norm(x, gain, bias):
    mu = jnp.mean(x, axis=-1, keepdims=True)
    xc = x - mu
    var = jnp.mean(xc * xc, axis=-1, keepdims=True)
    return xc * lax.rsqrt(var + EPS) * gain + bias


def _dot(a, b):
    return jnp.dot(a, b, preferred_element_type=F32)


def _ffn_body(x_ref, gain_ref, wg_ref, wu_ref, wd_ref, fgain_ref, o_ref, xn_ref, *, final_norm):
    k = pl.program_id(1)

    @pl.when(k == 0)
    def _():
        xn_ref[...] = _rms(x_ref[...], gain_ref[...]).astype(BF16)
        o_ref[...] = jnp.zeros_like(o_ref)

    xn = xn_ref[...]
    gate = _dot(xn, wg_ref[...])
    up = _dot(xn, wu_ref[...])
    act = (jax.nn.silu(gate) * up).astype(BF16)
    o_ref[...] += _dot(act, wd_ref[...])

    @pl.when(k == pl.num_programs(1) - 1)
    def _():
        y = x_ref[...] + 0.5 * o_ref[...]
        if final_norm:
            y = _rms(y, fgain_ref[...])
        o_ref[...] = y


def _ffn(x, gain, w_gu, w_down, fgain, *, tm, tf, final_norm):
    m, d = x.shape
    f = w_down.shape[0]
    nf = f // tf
    assert m % tm == 0 and f % tf == 0
    return pl.pallas_call(
        functools.partial(_ffn_body, final_norm=final_norm),
        grid=(m // tm, nf),
        in_specs=[
            pl.BlockSpec((tm, d), lambda i, k: (i, 0)),
            pl.BlockSpec((1, d), lambda i, k: (0, 0)),
            pl.BlockSpec((d, tf), lambda i, k: (0, k)),
            pl.BlockSpec((d, tf), lambda i, k: (0, k + nf)),
            pl.BlockSpec((tf, d), lambda i, k: (k, 0)),
            pl.BlockSpec((1, d), lambda i, k: (0, 0)),
        ],
        out_specs=pl.BlockSpec((tm, d), lambda i, k: (i, 0)),
        out_shape=jax.ShapeDtypeStruct((m, d), F32),
        scratch_shapes=[pltpu.VMEM((tm, d), BF16)],
        compiler_params=_params(),
        name="ffn",
    )(x, gain, w_gu, w_gu, w_down, fgain)


def _proj_sections(h, w_refs, u_ref, gv_ref, ga_ref, gb_ref):
    w_ua, w_va, w_glu_a, w_glu_b, w_gate_a, w_gate_b = w_refs
    u_ref[...] = jax.nn.gelu(_dot(h, w_ua[...])).astype(BF16)
    gv_ref[...] = jax.nn.gelu(_dot(h, w_va[...]))
    ga_ref[...] = jax.nn.sigmoid(_dot(h, w_gate_a[...])).astype(BF16)
    gb_ref[...] = jax.nn.sigmoid(_dot(h, w_gate_b[...])).astype(BF16)
    return _dot(h, w_glu_a[...]) * jax.nn.sigmoid(_dot(h, w_glu_b[...]))


def _proj_prompt_body(x_ref, gain_ref, w_ua, w_va, w_glu_a, w_glu_b, w_gate_a, w_gate_b,
                      wdw_ref, bdw_ref,
                      u_ref, gv_ref, c_ref, ga_ref, gb_ref, tail_ref,
                      h_ref, gext_ref, carry_ref, *, tiles_per_seq, conv_rows):
    i = pl.program_id(0)
    j = pl.program_id(1)
    tm = x_ref.shape[0]
    k_conv = wdw_ref.shape[0]

    @pl.when(j == 0)
    def _():
        h_ref[...] = _rms(x_ref[...], gain_ref[...]).astype(BF16)

    g = _proj_sections(h_ref[...], (w_ua, w_va, w_glu_a, w_glu_b, w_gate_a, w_gate_b),
                       u_ref, gv_ref, ga_ref, gb_ref)

    seq_start = (i % tiles_per_seq) == 0

    @pl.when(seq_start)
    def _():
        gext_ref[0:CARRY_ROWS, :] = jnp.zeros((CARRY_ROWS, gext_ref.shape[1]), F32)

    @pl.when(jnp.logical_not(seq_start))
    def _():
        gext_ref[0:CARRY_ROWS, :] = carry_ref[j]

    gext_ref[CARRY_ROWS:CARRY_ROWS + tm, :] = g
    g_last = g[tm - CARRY_ROWS:, :]
    carry_ref[j] = g_last
    tail_ref[0] = g_last

    shift = CARRY_ROWS - (k_conv - 1)
    bias = bdw_ref[...]
    for r in range(tm // conv_rows):
        acc = jnp.broadcast_to(bias, (conv_rows, bias.shape[1]))
        for k in range(k_conv):
            lo = r * conv_rows + shift + k
            acc = acc + wdw_ref[k:k + 1, :] * gext_ref[lo:lo + conv_rows, :]
        c_ref[r * conv_rows:(r + 1) * conv_rows, :] = acc


def _proj_sample_body(x_ref, gain_ref, w_ua, w_va, w_glu_a, w_glu_b, w_gate_a, w_gate_b,
                      wdw_ref, bdw_ref, cache_ref,
                      u_ref, gv_ref, c_ref, ga_ref, gb_ref, g_ref,
                      h_ref, *, n_seq):
    j = pl.program_id(1)
    k_conv = wdw_ref.shape[0]
    n_prev = k_conv - 1
    t_new = x_ref.shape[0] // n_seq

    @pl.when(j == 0)
    def _():
        h_ref[...] = _rms(x_ref[...], gain_ref[...]).astype(BF16)

    g = _proj_sections(h_ref[...], (w_ua, w_va, w_glu_a, w_glu_b, w_gate_a, w_gate_b),
                       u_ref, gv_ref, ga_ref, gb_ref)
    g_ref[...] = g

    bias = bdw_ref[...]
    for t in range(t_new):
        acc = jnp.broadcast_to(bias, (n_seq, bias.shape[1]))
        for k in range(k_conv):
            m = t + k
            if m < n_prev:
                slab = cache_ref[m]
            else:
                slab = g_ref[(m - n_prev) * n_seq:(m - n_prev + 1) * n_seq, :]
            acc = acc + wdw_ref[k:k + 1, :] * slab
        c_ref[t * n_seq:(t + 1) * n_seq, :] = acc


def _proj(x, gain, w_in, w_dw, b_dw, cache_t, *, tm, tn, seq_len, conv_rows):
    m, d = x.shape
    dsec = w_in.shape[1] // 6
    nblk = dsec // tn
    k_conv = w_dw.shape[0]
    assert m % tm == 0 and dsec % tn == 0
    w_specs = [pl.BlockSpec((d, tn), lambda i, j, s=s: (0, s * nblk + j)) for s in range(6)]
    in_specs = [
        pl.BlockSpec((tm, d), lambda i, j: (i, 0)),
        pl.BlockSpec((1, d), lambda i, j: (0, 0)),
        *w_specs,
        pl.BlockSpec((k_conv, tn), lambda i, j: (0, j)),
        pl.BlockSpec((1, tn), lambda i, j: (0, j)),
    ]
    tile = lambda i, j: (i, j)
    out_specs = [pl.BlockSpec((tm, tn), tile) for _ in range(5)]
    out_shape = [
        jax.ShapeDtypeStruct((m, dsec), BF16),
        jax.ShapeDtypeStruct((m, dsec), F32),
        jax.ShapeDtypeStruct((m, dsec), F32),
        jax.ShapeDtypeStruct((m, dsec), BF16),
        jax.ShapeDtypeStruct((m, dsec), BF16),
    ]
    scratch = [pltpu.VMEM((tm, d), BF16)]
    args = [x, gain, *([w_in] * 6), w_dw, b_dw]
    if cache_t is None:
        assert seq_len % tm == 0 and tm % conv_rows == 0 and tm >= CARRY_ROWS
        tiles_per_seq = seq_len // tm
        body = functools.partial(_proj_prompt_body, tiles_per_seq=tiles_per_seq,
                                 conv_rows=conv_rows)
        out_specs.append(pl.BlockSpec((1, CARRY_ROWS, tn), lambda i, j: (i, 0, j)))
        out_shape.append(jax.ShapeDtypeStruct((m // tm, CARRY_ROWS, dsec), F32))
        scratch += [pltpu.VMEM((CARRY_ROWS + tm, tn), F32),
                    pltpu.VMEM((nblk, CARRY_ROWS, tn), F32)]
        name = "proj_prompt"
    else:
        assert m == tm
        n_seq = cache_t.shape[1]
        body = functools.partial(_proj_sample_body, n_seq=n_seq)
        in_specs.append(pl.BlockSpec((k_conv - 1, n_seq, tn), lambda i, j: (0, 0, j)))
        args.append(cache_t)
        out_specs.append(pl.BlockSpec((tm, tn), tile))
        out_shape.append(jax.ShapeDtypeStruct((m, dsec), F32))
        name = "proj_sample"
    return pl.pallas_call(
        body,
        grid=(m // tm, nblk),
        in_specs=in_specs,
        out_specs=out_specs,
        out_shape=out_shape,
        scratch_shapes=scratch,
        compiler_params=_params(),
        name=name,
    )(*args)


def _mixa_prompt_body(u_ref, gv_ref, c_ref, ws_ref, bs_ref, lnvg_ref, lnvb_ref, lncg_ref, lncb_ref,
                      a_ref, cc_ref, vst_ref, vb_ref):
    tr = u_ref.shape[0]
    n_head, chunk, _ = ws_ref.shape
    hd = u_ref.shape[1] // n_head
    v = _layer_norm(gv_ref[...], lnvg_ref[...], lnvb_ref[...])
    vst_ref[0] = v[tr - chunk:, :]
    vb_ref[...] = v.astype(BF16)
    row = lax.broadcasted_iota(jnp.int32, (chunk, chunk), 0)
    col = lax.broadcasted_iota(jnp.int32, (chunk, chunk), 1)
    causal = (row >= col).astype(F32)
    for h in range(n_head):
        w = (ws_ref[h] * causal).astype(BF16)
        cols = slice(h * hd, (h + 1) * hd)
        for c in range(tr // chunk):
            rows = slice(c * chunk, (c + 1) * chunk)
            s = _dot(w, vb_ref[rows, cols]) + bs_ref[:, cols]
            a_ref[rows, cols] = (u_ref[rows, cols].astype(F32) * s).astype(BF16)
    cc_ref[...] = jax.nn.silu(_layer_norm(c_ref[...], lncg_ref[...], lncb_ref[...])).astype(BF16)


def _mixa_sample_body(u_ref, gv_ref, c_ref, coef_ref, bs_ref, lnvg_ref, lnvb_ref, lncg_ref,
                      lncb_ref, a_ref, cc_ref, v_ref, *, n_seq):
    t_new = u_ref.shape[0] // n_seq
    v_ref[...] = _layer_norm(gv_ref[...], lnvg_ref[...], lnvb_ref[...])
    for i in range(t_new):
        rows_i = slice(i * n_seq, (i + 1) * n_seq)
        s = jnp.broadcast_to(bs_ref[i:i + 1, :], (n_seq, bs_ref.shape[1]))
        for j in range(i + 1):
            s = s + coef_ref[i, j:j + 1, :] * v_ref[j * n_seq:(j + 1) * n_seq, :]
        a_ref[rows_i, :] = (u_ref[rows_i, :].astype(F32) * s).astype(BF16)
    cc_ref[...] = jax.nn.silu(_layer_norm(c_ref[...], lncg_ref[...], lncb_ref[...])).astype(BF16)


def _mixa_prompt(u, gv, c, w_s, bs_slab, ln_v_g, ln_v_b, ln_c_g, ln_c_b, *, tr, seq_len):
    m, d = u.shape
    n_head, chunk, _ = w_s.shape
    assert seq_len % tr == 0 and tr % chunk == 0
    tiles_per_seq = seq_len // tr
    row = lambda i: (i, 0)
    const2 = lambda i: (0, 0)
    return pl.pallas_call(
        _mixa_prompt_body,
        grid=(m // tr,),
        in_specs=[
            pl.BlockSpec((tr, d), row),
            pl.BlockSpec((tr, d), row),
            pl.BlockSpec((tr, d), row),
            pl.BlockSpec((n_head, chunk, chunk), lambda i: (0, 0, 0)),
            pl.BlockSpec((chunk, d), const2),
            pl.BlockSpec((1, d), const2),
            pl.BlockSpec((1, d), const2),
            pl.BlockSpec((1, d), const2),
            pl.BlockSpec((1, d), const2),
        ],
        out_specs=[
            pl.BlockSpec((tr, d), row),
            pl.BlockSpec((tr, d), row),
            pl.BlockSpec((1, chunk, d), lambda i: (i // tiles_per_seq, 0, 0)),
        ],
        out_shape=[
            jax.ShapeDtypeStruct((m, d), BF16),
            jax.ShapeDtypeStruct((m, d), BF16),
            jax.ShapeDtypeStruct((m // seq_len, chunk, d), F32),
        ],
        scratch_shapes=[pltpu.VMEM((tr, d), BF16)],
        compiler_params=pltpu.CompilerParams(dimension_semantics=("arbitrary",),
                                             vmem_limit_bytes=VMEM_LIMIT_BYTES),
        name="mixa_prompt",
    )(u, gv, c, w_s, bs_slab, ln_v_g, ln_v_b, ln_c_g, ln_c_b)


def _mixa_sample(u, gv, c, coef, bs_rows, ln_v_g, ln_v_b, ln_c_g, ln_c_b, *, n_seq):
    m, d = u.shape
    t_new = m // n_seq
    whole = lambda i: (0, 0)
    return pl.pallas_call(
        functools.partial(_mixa_sample_body, n_seq=n_seq),
        grid=(1,),
        in_specs=[
            pl.BlockSpec((m, d), whole),
            pl.BlockSpec((m, d), whole),
            pl.BlockSpec((m, d), whole),
            pl.BlockSpec((t_new, t_new, d), lambda i: (0, 0, 0)),
            pl.BlockSpec((t_new, d), whole),
            pl.BlockSpec((1, d), whole),
            pl.BlockSpec((1, d), whole),
            pl.BlockSpec((1, d), whole),
            pl.BlockSpec((1, d), whole),
        ],
        out_specs=[pl.BlockSpec((m, d), whole)] * 3,
        out_shape=[
            jax.ShapeDtypeStruct((m, d), BF16),
            jax.ShapeDtypeStruct((m, d), BF16),
            jax.ShapeDtypeStruct((m, d), F32),
        ],
        compiler_params=pltpu.CompilerParams(dimension_semantics=("arbitrary",),
                                             vmem_limit_bytes=VMEM_LIMIT_BYTES),
        name="mixa_sample",
    )(u, gv, c, coef, bs_rows, ln_v_g, ln_v_b, ln_c_g, ln_c_b)


def _mixb_body(x_ref, a_ref, cc_ref, ga_ref, gb_ref, wpa_ref, wpb_ref, wout_ref, o_ref):
    j = pl.program_id(1)

    @pl.when(j == 0)
    def _():
        o_ref[...] = jnp.zeros_like(o_ref)

    ya = _dot(a_ref[...], wpa_ref[...])
    yb = _dot(cc_ref[...], wpb_ref[...])
    merged = (ga_ref[...].astype(F32) * ya + gb_ref[...].astype(F32) * yb).astype(BF16)
    o_ref[...] += _dot(merged, wout_ref[...])

    @pl.when(j == pl.num_programs(1) - 1)
    def _():
        o_ref[...] = x_ref[...] + o_ref[...]


def _mixb(x, a, cc, ga, gb, w_pa, w_pb, w_out, *, tr, tj):
    m, d = x.shape
    dm = w_pa.shape[1]
    assert m % tr == 0 and dm % tj == 0
    row = lambda i, j: (i, 0)
    return pl.pallas_call(
        _mixb_body,
        grid=(m // tr, dm // tj),
        in_specs=[
            pl.BlockSpec((tr, d), row),
            pl.BlockSpec((tr, a.shape[1]), row),
            pl.BlockSpec((tr, cc.shape[1]), row),
            pl.BlockSpec((tr, tj), lambda i, j: (i, j)),
            pl.BlockSpec((tr, tj), lambda i, j: (i, j)),
            pl.BlockSpec((w_pa.shape[0], tj), lambda i, j: (0, j)),
            pl.BlockSpec((w_pb.shape[0], tj), lambda i, j: (0, j)),
            pl.BlockSpec((tj, d), lambda i, j: (j, 0)),
        ],
        out_specs=pl.BlockSpec((tr, d), row),
        out_shape=jax.ShapeDtypeStruct((m, d), F32),
        compiler_params=_params(),
        name="mixb",
    )(x, a, cc, ga, gb, w_pa, w_pb, w_out)


def kernel(x_prompt, x_sample, cache_conv, ffn1_norm, ffn1_w_gu, ffn1_w_down, mix_norm, w_in, w_s, b_s, ln_v_g, ln_v_b, w_pa, w_dw, b_dw, ln_c_g, ln_c_b, w_pb, w_out, ffn2_norm, ffn2_w_gu, ffn2_w_down, final_norm):
    n_p, t_p, d = x_prompt.shape
    n_s, t_s, _ = x_sample.shape
    depth, n_head, chunk, _ = w_s.shape
    d_a = ln_v_g.shape[1]
    hd = d_a // n_head
    k_conv = w_dw.shape[1]
    assert t_s <= chunk and k_conv - 1 <= CARRY_ROWS and n_s % SUBLANES == 0
    last_len = t_p - chunk * ((t_p - 1) // chunk)
    assert last_len == chunk, "prompt length must be a whole number of chunks"

    xp = x_prompt.reshape(n_p * t_p, d)
    xs = jnp.swapaxes(x_sample, 0, 1).reshape(t_s * n_s, d)
    m_s = t_s * n_s
    row2 = lambda v: v.reshape(1, -1)
    fgain = row2(final_norm)

    vp_list, vs_list, cp_list, cs_list = [], [], [], []
    for l in range(depth):
        last = l == depth - 1
        w1gu, w1d = ffn1_w_gu[l].astype(BF16), ffn1_w_down[l].astype(BF16)
        w2gu, w2d = ffn2_w_gu[l].astype(BF16), ffn2_w_down[l].astype(BF16)
        win = w_in[l].astype(BF16)
        wpa, wpb, wout = w_pa[l].astype(BF16), w_pb[l].astype(BF16), w_out[l].astype(BF16)
        bdw = row2(b_dw[l])
        lnvg, lnvb, lncg, lncb = row2(ln_v_g[l]), row2(ln_v_b[l]), row2(ln_c_g[l]), row2(ln_c_b[l])
        bs_slab = jnp.repeat(b_s[l].T, hd, axis=1)
        coef = jnp.repeat(jnp.transpose(w_s[l][:, :t_s, :t_s], (1, 2, 0)), hd, axis=2)
        cache_t = jnp.swapaxes(cache_conv[l], 0, 1)

        xp = _ffn(xp, row2(ffn1_norm[l]), w1gu, w1d, fgain, tm=1024, tf=256, final_norm=False)
        tm_proj = 512
        u, gv, c, ga, gb, tail = _proj(xp, row2(mix_norm[l]), win, w_dw[l], bdw, None,
                                       tm=tm_proj, tn=256, seq_len=t_p, conv_rows=64)
        tail = tail[t_p // tm_proj - 1::t_p // tm_proj]
        a, cc, v_p = _mixa_prompt(u, gv, c, w_s[l], bs_slab, lnvg, lnvb, lncg, lncb,
                                  tr=256, seq_len=t_p)
        xp = _mixb(xp, a, cc, ga, gb, wpa, wpb, wout, tr=1024, tj=256)
        xp = _ffn(xp, row2(ffn2_norm[l]), w2gu, w2d, fgain, tm=1024, tf=256, final_norm=last)
        vp_list.append(v_p)
        cp_list.append(tail[:, CARRY_ROWS - (k_conv - 1):, :])

        xs = _ffn(xs, row2(ffn1_norm[l]), w1gu, w1d, fgain, tm=m_s, tf=256, final_norm=False)
        u, gv, c, ga, gb, g_s = _proj(xs, row2(mix_norm[l]), win, w_dw[l], bdw, cache_t,
                                      tm=m_s, tn=256, seq_len=t_s, conv_rows=64)
        a, cc, v_s = _mixa_sample(u, gv, c, coef, bs_slab[:t_s], lnvg, lnvb, lncg, lncb, n_seq=n_s)
        xs = _mixb(xs, a, cc, ga, gb, wpa, wpb, wout, tr=m_s, tj=256)
        xs = _ffn(xs, row2(ffn2_norm[l]), w2gu, w2d, fgain, tm=m_s, tf=256, final_norm=last)
        vs_list.append(jnp.swapaxes(v_s.reshape(t_s, n_s, d_a), 0, 1))
        g_seq = jnp.swapaxes(g_s.reshape(t_s, n_s, -1), 0, 1)
        cs_list.append(jnp.concatenate([cache_conv[l], g_seq], axis=1)[:, -(k_conv - 1):])

    y_prompt = xp.reshape(n_p, t_p, d)
    y_sample = jnp.swapaxes(xs.reshape(t_s, n_s, d), 0, 1)
    return (y_prompt, y_sample, jnp.stack(vp_list), jnp.stack(vs_list),
            jnp.stack(cp_list), jnp.stack(cs_list))
```

```python
import functools

import jax
import jax.numpy as jnp
from jax import lax
from jax.experimental import pallas as pl
from jax.experimental.pallas import tpu as pltpu

F32 = jnp.float32
BF16 = jnp.bfloat16
EPS = 1e-6

V7X_VMEM_BYTES = 64 * 1024 * 1024
VMEM_LIMIT_BYTES = V7X_VMEM_BYTES - 4 * 1024 * 1024
SUBLANES = 8
LANES = 128
CONV_STRIDE = 2
CONV_GROUPS = 4
CARRY_ROWS = 32


def _params():
    return pltpu.CompilerParams(
        dimension_semantics=("arbitrary", "arbitrary"),
        vmem_limit_bytes=VMEM_LIMIT_BYTES,
    )


def _rms(x, gain):
    ms = jnp.mean(x * x, axis=-1, keepdims=True)
    return x * lax.rsqrt(ms + EPS) * gain


def _layer_norm(x, gain, bias):
    mu = jnp.mean(x, axis=-1, keepdims=True)
    xc = x - mu
    var = jnp.mean(xc * xc, axis=-1, keepdims=True)
    return xc * lax.rsqrt(var + EPS) * gain + bias


def _dot(a, b):
    return jnp.dot(a, b, preferred_element_type=F32)


def _ffn_body(x_ref, gain_ref, wg_ref, wu_ref, wd_ref, fgain_ref, o_ref, xn_ref, *, final_norm):
    k = pl.program_id(1)

    @pl.when(k == 0)
    def _():
        xn_ref[...] = _rms(x_ref[...], gain_ref[...]).astype(BF16)
        o_ref[...] = jnp.zeros_like(o_ref)

    xn = xn_ref[...]
    gate = _dot(xn, wg_ref[...])
    up = _dot(xn, wu_ref[...])
    act = (jax.nn.silu(gate) * up).astype(BF16)
    o_ref[...] += _dot(act, wd_ref[...])

    @pl.when(k == pl.num_programs(1) - 1)
    def _():
        y = x_ref[...] + 0.5 * o_ref[...]
        if final_norm:
            y = _rms(y, fgain_ref[...])
        o_ref[...] = y


def _ffn(x, gain, w_gu, w_down, fgain, *, tm, tf, final_norm):
    m, d = x.shape
    f = w_down.shape[0]
    nf = f // tf
    assert m % tm == 0 and f % tf == 0
    return pl.pallas_call(
        functools.partial(_ffn_body, final_norm=final_norm),
        grid=(m // tm, nf),
        in_specs=[
            pl.BlockSpec((tm, d), lambda i, k: (i, 0)),
            pl.BlockSpec((1, d), lambda i, k: (0, 0)),
            pl.BlockSpec((d, tf), lambda i, k: (0, k)),
            pl.BlockSpec((d, tf), lambda i, k: (0, k + nf)),
            pl.BlockSpec((tf, d), lambda i, k: (k, 0)),
            pl.BlockSpec((1, d), lambda i, k: (0, 0)),
        ],
        out_specs=pl.BlockSpec((tm, d), lambda i, k: (i, 0)),
        out_shape=jax.ShapeDtypeStruct((m, d), F32),
        scratch_shapes=[pltpu.VMEM((tm, d), BF16)],
        compiler_params=_params(),
        name="ffn",
    )(x, gain, w_gu, w_gu, w_down, fgain)


def _glu(h, w_glu_a, w_glu_b):
    return _dot(h, w_glu_a[...]) * jax.nn.sigmoid(_dot(h, w_glu_b[...]))


def _proj_sections(h, w_ua, w_va, w_gate_a, w_gate_b, u_ref, gv_ref, ga_ref, gb_ref):
    u_ref[...] = jax.nn.gelu(_dot(h, w_ua[...])).astype(BF16)
    gv_ref[...] = jax.nn.gelu(_dot(h, w_va[...]))
    ga_ref[...] = jax.nn.sigmoid(_dot(h, w_gate_a[...])).astype(BF16)
    gb_ref[...] = jax.nn.sigmoid(_dot(h, w_gate_b[...])).astype(BF16)


def _causal_conv(gext_ref, cout_ref, wdw_ref, bdw_ref, c_ref):
    n_slab, tm, _ = cout_ref.shape
    k_conv = wdw_ref.shape[0]
    shift = CARRY_ROWS - (k_conv - 1)
    group = CONV_STRIDE * SUBLANES
    starts = [b * group + p for b in range(CONV_GROUPS) for p in range(CONV_STRIDE)]
    for s in range(n_slab):
        lanes = slice(s * LANES, (s + 1) * LANES)
        bias = jnp.broadcast_to(bdw_ref[:, lanes], (SUBLANES, LANES))
        for r0 in range(0, tm, CONV_GROUPS * group):
            accs = [bias for _ in starts]
            for k in range(k_conv):
                wk = jnp.broadcast_to(wdw_ref[k:k + 1, lanes], (SUBLANES, LANES))
                for q, st in enumerate(starts):
                    rows = pl.ds(r0 + st + shift + k, SUBLANES, stride=CONV_STRIDE)
                    accs[q] = accs[q] + wk * gext_ref[s, rows, :]
            for q, st in enumerate(starts):
                cout_ref[s, pl.ds(r0 + st, SUBLANES, stride=CONV_STRIDE), :] = accs[q]
        c_ref[:, lanes] = cout_ref[s]


def _proj_prompt_body(x_ref, gain_ref, w_ua, w_va, w_glu_a, w_glu_b, w_gate_a, w_gate_b,
                      wdw_ref, bdw_ref,
                      u_ref, gv_ref, c_ref, ga_ref, gb_ref, tail_ref,
                      h_ref, raw_even_ref, raw_odd_ref, gext_ref, cout_ref, carry_ref,
                      *, n_steps, nblk, tiles_per_seq):
    s = pl.program_id(0)
    tm = x_ref.shape[0]
    n_slab = gext_ref.shape[0]
    j = jnp.minimum(s, n_steps - 1) % nblk
    prev = jnp.maximum(s - 1, 0)
    prev_i = prev // nblk
    prev_j = prev % nblk

    @pl.when(jnp.logical_and(j == 0, s < n_steps))
    def _():
        h_ref[...] = _rms(x_ref[...], gain_ref[...]).astype(BF16)

    @pl.when(s == 0)
    def _():
        raw_odd_ref[...] = jnp.zeros_like(raw_odd_ref)

    @pl.when(jnp.logical_and(prev_j == 0, prev_i % tiles_per_seq == 0))
    def _():
        carry_ref[...] = jnp.zeros_like(carry_ref)

    def step(raw_w, raw_r):
        h = h_ref[...]
        u_ref[...] = jax.nn.gelu(raw_r[0]).astype(BF16)
        gv_ref[...] = jax.nn.gelu(raw_r[1])
        ga_ref[...] = jax.nn.sigmoid(raw_r[4]).astype(BF16)
        gb_ref[...] = jax.nn.sigmoid(raw_r[5]).astype(BF16)
        g = raw_r[2] * jax.nn.sigmoid(raw_r[3])
        gext_ref[:, 0:CARRY_ROWS, :] = carry_ref[prev_j]
        for sl in range(n_slab):
            gext_ref[sl, CARRY_ROWS:CARRY_ROWS + tm, :] = g[:, sl * LANES:(sl + 1) * LANES]
            carry_ref[prev_j, sl] = g[tm - CARRY_ROWS:, sl * LANES:(sl + 1) * LANES]
        tail_ref[0] = g[tm - CARRY_ROWS:, :]
        _causal_conv(gext_ref, cout_ref, wdw_ref, bdw_ref, c_ref)

        for sec, w in enumerate((w_ua, w_va, w_glu_a, w_glu_b, w_gate_a, w_gate_b)):
            raw_w[sec] = _dot(h, w[...])

    @pl.when(s % 2 == 0)
    def _():
        step(raw_even_ref, raw_odd_ref)

    @pl.when(s % 2 == 1)
    def _():
        step(raw_odd_ref, raw_even_ref)


def _proj_sample_body(x_ref, gain_ref, w_ua, w_va, w_glu_a, w_glu_b, w_gate_a, w_gate_b,
                      wdw_ref, bdw_ref, cache_ref,
                      u_ref, gv_ref, c_ref, ga_ref, gb_ref, g_ref,
                      h_ref, *, n_seq):
    j = pl.program_id(1)
    k_conv = wdw_ref.shape[0]
    n_prev = k_conv - 1
    t_new = x_ref.shape[0] // n_seq

    @pl.when(j == 0)
    def _():
        h_ref[...] = _rms(x_ref[...], gain_ref[...]).astype(BF16)

    h = h_ref[...]
    g_ref[...] = _glu(h, w_glu_a, w_glu_b)
    _proj_sections(h, w_ua, w_va, w_gate_a, w_gate_b, u_ref, gv_ref, ga_ref, gb_ref)

    bias = bdw_ref[...]
    for t in range(t_new):
        acc = jnp.broadcast_to(bias, (n_seq, bias.shape[1]))
        for k in range(k_conv):
            m = t + k
            if m < n_prev:
                slab = cache_ref[m]
            else:
                slab = g_ref[(m - n_prev) * n_seq:(m - n_prev + 1) * n_seq, :]
            acc = acc + wdw_ref[k:k + 1, :] * slab
        c_ref[t * n_seq:(t + 1) * n_seq, :] = acc


def _proj_out_shapes(m, dsec):
    return [
        jax.ShapeDtypeStruct((m, dsec), BF16),
        jax.ShapeDtypeStruct((m, dsec), F32),
        jax.ShapeDtypeStruct((m, dsec), F32),
        jax.ShapeDtypeStruct((m, dsec), BF16),
        jax.ShapeDtypeStruct((m, dsec), BF16),
    ]


def _proj_prompt(x, gain, w_in, w_dw, b_dw, *, tm, tn, seq_len):
    m, d = x.shape
    dsec = w_in.shape[1] // 6
    nblk = dsec // tn
    n_slab = tn // LANES
    k_conv = w_dw.shape[0]
    assert m % tm == 0 and dsec % tn == 0 and tn % LANES == 0
    assert seq_len % tm == 0 and tm >= CARRY_ROWS
    assert tm % (CONV_GROUPS * CONV_STRIDE * SUBLANES) == 0
    n_steps = (m // tm) * nblk
    cur = lambda s: jnp.minimum(s, n_steps - 1)
    prev = lambda s: jnp.maximum(s - 1, 0)
    w_specs = [pl.BlockSpec((d, tn), lambda s, sec=sec: (0, sec * nblk + cur(s) % nblk))
               for sec in range(6)]
    out_tile = lambda s: (prev(s) // nblk, prev(s) % nblk)
    return pl.pallas_call(
        functools.partial(_proj_prompt_body, n_steps=n_steps, nblk=nblk,
                          tiles_per_seq=seq_len // tm),
        grid=(n_steps + 1,),
        in_specs=[
            pl.BlockSpec((tm, d), lambda s: (cur(s) // nblk, 0)),
            pl.BlockSpec((1, d), lambda s: (0, 0)),
            *w_specs,
            pl.BlockSpec((k_conv, tn), lambda s: (0, prev(s) % nblk)),
            pl.BlockSpec((1, tn), lambda s: (0, prev(s) % nblk)),
        ],
        out_specs=[pl.BlockSpec((tm, tn), out_tile) for _ in range(5)]
        + [pl.BlockSpec((1, CARRY_ROWS, tn), lambda s: (prev(s) // nblk, 0, prev(s) % nblk))],
        out_shape=_proj_out_shapes(m, dsec)
        + [jax.ShapeDtypeStruct((m // tm, CARRY_ROWS, dsec), F32)],
        scratch_shapes=[
            pltpu.VMEM((tm, d), BF16),
            pltpu.VMEM((6, tm, tn), F32),
            pltpu.VMEM((6, tm, tn), F32),
            pltpu.VMEM((n_slab, CARRY_ROWS + tm, LANES), F32),
            pltpu.VMEM((n_slab, tm, LANES), F32),
            pltpu.VMEM((nblk, n_slab, CARRY_ROWS, LANES), F32),
        ],
        compiler_params=pltpu.CompilerParams(
            dimension_semantics=("arbitrary",), vmem_limit_bytes=VMEM_LIMIT_BYTES,
        ),
        name="proj_prompt",
    )(x, gain, *([w_in] * 6), w_dw, b_dw)


def _proj_sample(x, gain, w_in, w_dw, b_dw, cache_t, *, tn):
    m, d = x.shape
    dsec = w_in.shape[1] // 6
    nblk = dsec // tn
    k_conv = w_dw.shape[0]
    n_seq = cache_t.shape[1]
    assert dsec % tn == 0 and m % n_seq == 0
    w_specs = [pl.BlockSpec((d, tn), lambda i, j, sec=sec: (0, sec * nblk + j)) for sec in range(6)]
    tile = lambda i, j: (i, j)
    return pl.pallas_call(
        functools.partial(_proj_sample_body, n_seq=n_seq),
        grid=(1, nblk),
        in_specs=[
            pl.BlockSpec((m, d), lambda i, j: (i, 0)),
            pl.BlockSpec((1, d), lambda i, j: (0, 0)),
            *w_specs,
            pl.BlockSpec((k_conv, tn), lambda i, j: (0, j)),
            pl.BlockSpec((1, tn), lambda i, j: (0, j)),
            pl.BlockSpec((k_conv - 1, n_seq, tn), lambda i, j: (0, 0, j)),
        ],
        out_specs=[pl.BlockSpec((m, tn), tile) for _ in range(6)],
        out_shape=_proj_out_shapes(m, dsec) + [jax.ShapeDtypeStruct((m, dsec), F32)],
        scratch_shapes=[pltpu.VMEM((m, d), BF16)],
        compiler_params=_params(),
        name="proj_sample",
    )(x, gain, *([w_in] * 6), w_dw, b_dw, cache_t)


def _mixa_prompt_body(u_ref, gv_ref, c_ref, ws_ref, bs_ref, lnvg_ref, lnvb_ref, lncg_ref, lncb_ref,
                      a_ref, cc_ref, vst_ref, vb_ref):
    tr = u_ref.shape[0]
    n_head, chunk, _ = ws_ref.shape
    hd = u_ref.shape[1] // n_head
    v = _layer_norm(gv_ref[...], lnvg_ref[...], lnvb_ref[...])
    vst_ref[0] = v[tr - chunk:, :]
    vb_ref[...] = v.astype(BF16)
    row = lax.broadcasted_iota(jnp.int32, (chunk, chunk), 0)
    col = lax.broadcasted_iota(jnp.int32, (chunk, chunk), 1)
    causal = (row >= col).astype(F32)
    for h in range(n_head):
        w = (ws_ref[h] * causal).astype(BF16)
        cols = slice(h * hd, (h + 1) * hd)
        for c in range(tr // chunk):
            rows = slice(c * chunk, (c + 1) * chunk)
            s = _dot(w, vb_ref[rows, cols]) + bs_ref[:, cols]
            a_ref[rows, cols] = (u_ref[rows, cols].astype(F32) * s).astype(BF16)
    cc_ref[...] = jax.nn.silu(_layer_norm(c_ref[...], lncg_ref[...], lncb_ref[...])).astype(BF16)


def _mixa_sample_body(u_ref, gv_ref, c_ref, coef_ref, bs_ref, lnvg_ref, lnvb_ref, lncg_ref,
                      lncb_ref, a_ref, cc_ref, v_ref, *, n_seq):
    t_new = u_ref.shape[0] // n_seq
    v_ref[...] = _layer_norm(gv_ref[...], lnvg_ref[...], lnvb_ref[...])
    for i in range(t_new):
        rows_i = slice(i * n_seq, (i + 1) * n_seq)
        s = jnp.broadcast_to(bs_ref[i:i + 1, :], (n_seq, bs_ref.shape[1]))
        for j in range(i + 1):
            s = s + coef_ref[i, j:j + 1, :] * v_ref[j * n_seq:(j + 1) * n_seq, :]
        a_ref[rows_i, :] = (u_ref[rows_i, :].astype(F32) * s).astype(BF16)
    cc_ref[...] = jax.nn.silu(_layer_norm(c_ref[...], lncg_ref[...], lncb_ref[...])).astype(BF16)


def _mixa_prompt(u, gv, c, w_s, bs_slab, ln_v_g, ln_v_b, ln_c_g, ln_c_b, *, tr, seq_len):
    m, d = u.shape
    n_head, chunk, _ = w_s.shape
    assert seq_len % tr == 0 and tr % chunk == 0
    tiles_per_seq = seq_len // tr
    row = lambda i: (i, 0)
    const2 = lambda i: (0, 0)
    return pl.pallas_call(
        _mixa_prompt_body,
        grid=(m // tr,),
        in_specs=[
            pl.BlockSpec((tr, d), row),
            pl.BlockSpec((tr, d), row),
            pl.BlockSpec((tr, d), row),
            pl.BlockSpec((n_head, chunk, chunk), lambda i: (0, 0, 0)),
            pl.BlockSpec((chunk, d), const2),
            pl.BlockSpec((1, d), const2),
            pl.BlockSpec((1, d), const2),
            pl.BlockSpec((1, d), const2),
            pl.BlockSpec((1, d), const2),
        ],
        out_specs=[
            pl.BlockSpec((tr, d), row),
            pl.BlockSpec((tr, d), row),
            pl.BlockSpec((1, chunk, d), lambda i: (i // tiles_per_seq, 0, 0)),
        ],
        out_shape=[
            jax.ShapeDtypeStruct((m, d), BF16),
            jax.ShapeDtypeStruct((m, d), BF16),
            jax.ShapeDtypeStruct((m // seq_len, chunk, d), F32),
        ],
        scratch_shapes=[pltpu.VMEM((tr, d), BF16)],
        compiler_params=pltpu.CompilerParams(dimension_semantics=("arbitrary",),
                                             vmem_limit_bytes=VMEM_LIMIT_BYTES),
        name="mixa_prompt",
    )(u, gv, c, w_s, bs_slab, ln_v_g, ln_v_b, ln_c_g, ln_c_b)


def _mixa_sample(u, gv, c, coef, bs_rows, ln_v_g, ln_v_b, ln_c_g, ln_c_b, *, n_seq):
    m, d = u.shape
    t_new = m // n_seq
    whole = lambda i: (0, 0)
    return pl.pallas_call(
        functools.partial(_mixa_sample_body, n_seq=n_seq),
        grid=(1,),
        in_specs=[
            pl.BlockSpec((m, d), whole),
            pl.BlockSpec((m, d), whole),
            pl.BlockSpec((m, d), whole),
            pl.BlockSpec((t_new, t_new, d), lambda i: (0, 0, 0)),
            pl.BlockSpec((t_new, d), whole),
            pl.BlockSpec((1, d), whole),
            pl.BlockSpec((1, d), whole),
            pl.BlockSpec((1, d), whole),
            pl.BlockSpec((1, d), whole),
        ],
        out_specs=[pl.BlockSpec((m, d), whole)] * 3,
        out_shape=[
            jax.ShapeDtypeStruct((m, d), BF16),
            jax.ShapeDtypeStruct((m, d), BF16),
            jax.ShapeDtypeStruct((m, d), F32),
        ],
        compiler_params=pltpu.CompilerParams(dimension_semantics=("arbitrary",),
                                             vmem_limit_bytes=VMEM_LIMIT_BYTES),
        name="mixa_sample",
    )(u, gv, c, coef, bs_rows, ln_v_g, ln_v_b, ln_c_g, ln_c_b)


def _mixb_body(x_ref, a_ref, cc_ref, ga_ref, gb_ref, wpa_ref, wpb_ref, wout_ref, o_ref):
    j = pl.program_id(1)

    @pl.when(j == 0)
    def _():
        o_ref[...] = jnp.zeros_like(o_ref)

    ya = _dot(a_ref[...], wpa_ref[...])
    yb = _dot(cc_ref[...], wpb_ref[...])
    merged = (ga_ref[...].astype(F32) * ya + gb_ref[...].astype(F32) * yb).astype(BF16)
    o_ref[...] += _dot(merged, wout_ref[...])

    @pl.when(j == pl.num_programs(1) - 1)
    def _():
        o_ref[...] = x_ref[...] + o_ref[...]


def _mixb(x, a, cc, ga, gb, w_pa, w_pb, w_out, *, tr, tj):
    m, d = x.shape
    dm = w_pa.shape[1]
    assert m % tr == 0 and dm % tj == 0
    row = lambda i, j: (i, 0)
    return pl.pallas_call(
        _mixb_body,
        grid=(m // tr, dm // tj),
        in_specs=[
            pl.BlockSpec((tr, d), row),
            pl.BlockSpec((tr, a.shape[1]), row),
            pl.BlockSpec((tr, cc.shape[1]), row),
            pl.BlockSpec((tr, tj), lambda i, j: (i, j)),
            pl.BlockSpec((tr, tj), lambda i, j: (i, j)),
            pl.BlockSpec((w_pa.shape[0], tj), lambda i, j: (0, j)),
            pl.BlockSpec((w_pb.shape[0], tj), lambda i, j: (0, j)),
            pl.BlockSpec((tj, d), lambda i, j: (j, 0)),
        ],
        out_specs=pl.BlockSpec((tr, d), row),
        out_shape=jax.ShapeDtypeStruct((m, d), F32),
        compiler_params=_params(),
        name="mixb",
    )(x, a, cc, ga, gb, w_pa, w_pb, w_out)


def kernel(x_prompt, x_sample, cache_conv, ffn1_norm, ffn1_w_gu, ffn1_w_down, mix_norm, w_in, w_s, b_s, ln_v_g, ln_v_b, w_pa, w_dw, b_dw, ln_c_g, ln_c_b, w_pb, w_out, ffn2_norm, ffn2_w_gu, ffn2_w_down, final_norm):
    n_p, t_p, d = x_prompt.shape
    n_s, t_s, _ = x_sample.shape
    depth, n_head, chunk, _ = w_s.shape
    d_a = ln_v_g.shape[1]
    hd = d_a // n_head
    k_conv = w_dw.shape[1]
    assert t_s <= chunk and k_conv - 1 <= CARRY_ROWS and n_s % SUBLANES == 0
    last_len = t_p - chunk * ((t_p - 1) // chunk)
    assert last_len == chunk, "prompt length must be a whole number of chunks"

    xp = x_prompt.reshape(n_p * t_p, d)
    xs = jnp.swapaxes(x_sample, 0, 1).reshape(t_s * n_s, d)
    m_s = t_s * n_s
    row2 = lambda v: v.reshape(1, -1)
    fgain = row2(final_norm)

    vp_list, vs_list, cp_list, cs_list = [], [], [], []
    for l in range(depth):
        last = l == depth - 1
        w1gu, w1d = ffn1_w_gu[l].astype(BF16), ffn1_w_down[l].astype(BF16)
        w2gu, w2d = ffn2_w_gu[l].astype(BF16), ffn2_w_down[l].astype(BF16)
        win = w_in[l].astype(BF16)
        wpa, wpb, wout = w_pa[l].astype(BF16), w_pb[l].astype(BF16), w_out[l].astype(BF16)
        bdw = row2(b_dw[l])
        lnvg, lnvb, lncg, lncb = row2(ln_v_g[l]), row2(ln_v_b[l]), row2(ln_c_g[l]), row2(ln_c_b[l])
        bs_slab = jnp.repeat(b_s[l].T, hd, axis=1)
        coef = jnp.repeat(jnp.transpose(w_s[l][:, :t_s, :t_s], (1, 2, 0)), hd, axis=2)
        cache_t = jnp.swapaxes(cache_conv[l], 0, 1)

        xp = _ffn(xp, row2(ffn1_norm[l]), w1gu, w1d, fgain, tm=1024, tf=256, final_norm=False)
        tm_proj = 512
        u, gv, c, ga, gb, tail = _proj_prompt(xp, row2(mix_norm[l]), win, w_dw[l], bdw,
                                              tm=tm_proj, tn=256, seq_len=t_p)
        tail = tail[t_p // tm_proj - 1::t_p // tm_proj]
        a, cc, v_p = _mixa_prompt(u, gv, c, w_s[l], bs_slab, lnvg, lnvb, lncg, lncb,
                                  tr=256, seq_len=t_p)
        xp = _mixb(xp, a, cc, ga, gb, wpa, wpb, wout, tr=1024, tj=256)
        xp = _ffn(xp, row2(ffn2_norm[l]), w2gu, w2d, fgain, tm=1024, tf=256, final_norm=last)
        vp_list.append(v_p)
        cp_list.append(tail[:, CARRY_ROWS - (k_conv - 1):, :])

        xs = _ffn(xs, row2(ffn1_norm[l]), w1gu, w1d, fgain, tm=m_s, tf=256, final_norm=False)
        u, gv, c, ga, gb, g_s = _proj_sample(xs, row2(mix_norm[l]), win, w_dw[l], bdw, cache_t,
                                             tn=256)
        a, cc, v_s = _mixa_sample(u, gv, c, coef, bs_slab[:t_s], lnvg, lnvb, lncg, lncb, n_seq=n_s)
        xs = _mixb(xs, a, cc, ga, gb, wpa, wpb, wout, tr=m_s, tj=256)
        xs = _ffn(xs, row2(ffn2_norm[l]), w2gu, w2d, fgain, tm=m_s, tf=256, final_norm=last)
        vs_list.append(jnp.swapaxes(v_s.reshape(t_s, n_s, d_a), 0, 1))
        g_seq = jnp.swapaxes(g_s.reshape(t_s, n_s, -1), 0, 1)
        cs_list.append(jnp.concatenate([cache_conv[l], g_seq], axis=1)[:, -(k_conv - 1):])

    y_prompt = xp.reshape(n_p, t_p, d)
    y_sample = jnp.swapaxes(xs.reshape(t_s, n_s, d), 0, 1)
    return (y_prompt, y_sample, jnp.stack(vp_list), jnp.stack(vs_list),
            jnp.stack(cp_list), jnp.stack(cs_list))
```

```python
import functools
from typing import NamedTuple

import jax
import jax.numpy as jnp
from jax import lax
from jax.experimental import pallas as pl
from jax.experimental.pallas import tpu as pltpu

F32 = jnp.float32
BF16 = jnp.bfloat16
EPS = 1e-6

V7X_VMEM_BYTES = 64 * 1024 * 1024
VMEM_LIMIT_BYTES = V7X_VMEM_BYTES - 4 * 1024 * 1024
SUBLANES = 8
LANES = 128
CONV_STRIDE = 2
CONV_GROUPS = 4
CARRY_ROWS = 32
ROW_CHUNK = 64


def _params():
    return pltpu.CompilerParams(
        dimension_semantics=("arbitrary", "arbitrary"),
        vmem_limit_bytes=VMEM_LIMIT_BYTES,
    )


class _Tiles(NamedTuple):
    ffn_rows: int
    ffn_hidden: int
    proj_rows: int
    proj_cols: int
    mixa_rows: int
    mix_rows: int
    mix_cols: int


def _tile_sizes():
    return _Tiles(ffn_rows=1024, ffn_hidden=512, proj_rows=512, proj_cols=256,
                  mixa_rows=256, mix_rows=1024, mix_cols=256)


def _rms(x, gain):
    ms = jnp.mean(x * x, axis=-1, keepdims=True)
    return x * lax.rsqrt(ms + EPS) * gain


def _layer_norm(x, gain, bias):
    mu = jnp.mean(x, axis=-1, keepdims=True)
    xc = x - mu
    var = jnp.mean(xc * xc, axis=-1, keepdims=True)
    return xc * lax.rsqrt(var + EPS) * gain + bias


def _dot(a, b):
    return jnp.dot(a, b, preferred_element_type=F32)


def _tile_columns(w, n_sec, tn):
    d, total = w.shape
    n = total // n_sec
    assert n % tn == 0
    return jnp.transpose(w.reshape(d, n_sec, n // tn, tn), (2, 1, 0, 3)).astype(BF16)


def _ffn_body(x_ref, gain_ref, wgu_ref, wd_ref, fgain_ref, o_ref, xn_ref, *, final_norm):
    k = pl.program_id(1)

    n_chunks = x_ref.shape[0] // ROW_CHUNK

    def chunk_rows(c):
        return pl.ds(pl.multiple_of(c * ROW_CHUNK, ROW_CHUNK), ROW_CHUNK)

    @pl.when(k == 0)
    def _():
        def norm_chunk(c, carry):
            rows = chunk_rows(c)
            xn_ref[rows, :] = _rms(x_ref[rows, :], gain_ref[...]).astype(BF16)
            return carry

        lax.fori_loop(0, n_chunks, norm_chunk, 0)
        o_ref[...] = jnp.zeros_like(o_ref)

    xn = xn_ref[...]
    gate = _dot(xn, wgu_ref[0, 0])
    up = _dot(xn, wgu_ref[0, 1])
    act = (jax.nn.silu(gate) * up).astype(BF16)
    o_ref[...] += _dot(act, wd_ref[...])

    @pl.when(k == pl.num_programs(1) - 1)
    def _():
        def finish_chunk(c, carry):
            rows = chunk_rows(c)
            y = x_ref[rows, :] + 0.5 * o_ref[rows, :]
            if final_norm:
                y = _rms(y, fgain_ref[...])
            o_ref[rows, :] = y
            return carry

        lax.fori_loop(0, n_chunks, finish_chunk, 0)


def _ffn(x, gain, w_gu_tiles, w_down, fgain, *, tm, final_norm):
    m, d = x.shape
    nf, _, _, tf = w_gu_tiles.shape
    assert m % tm == 0 and w_down.shape[0] == nf * tf
    return pl.pallas_call(
        functools.partial(_ffn_body, final_norm=final_norm),
        grid=(m // tm, nf),
        in_specs=[
            pl.BlockSpec((tm, d), lambda i, k: (i, 0)),
            pl.BlockSpec((1, d), lambda i, k: (0, 0)),
            pl.BlockSpec((1, 2, d, tf), lambda i, k: (k, 0, 0, 0)),
            pl.BlockSpec((tf, d), lambda i, k: (k, 0)),
            pl.BlockSpec((1, d), lambda i, k: (0, 0)),
        ],
        out_specs=pl.BlockSpec((tm, d), lambda i, k: (i, 0)),
        out_shape=jax.ShapeDtypeStruct((m, d), F32),
        scratch_shapes=[pltpu.VMEM((tm, d), BF16)],
        compiler_params=_params(),
        name="ffn",
    )(x, gain, w_gu_tiles, w_down, fgain)


SEC_UA, SEC_VA, SEC_GLU_A, SEC_GLU_B, SEC_GATE_A, SEC_GATE_B = range(6)
N_SEC = 6


def _section_outputs(pre, u_ref, gv_ref, ga_ref, gb_ref):
    u_ref[...] = jax.nn.gelu(pre(SEC_UA)).astype(BF16)
    gv_ref[...] = jax.nn.gelu(pre(SEC_VA))
    ga_ref[...] = jax.nn.sigmoid(pre(SEC_GATE_A)).astype(BF16)
    gb_ref[...] = jax.nn.sigmoid(pre(SEC_GATE_B)).astype(BF16)
    return pre(SEC_GLU_A) * jax.nn.sigmoid(pre(SEC_GLU_B))


def _causal_conv(gext_ref, cout_ref, wdw_ref, bdw_ref, c_ref):
    n_slab, tm, _ = cout_ref.shape
    k_conv = wdw_ref.shape[0]
    shift = CARRY_ROWS - (k_conv - 1)
    group = CONV_STRIDE * SUBLANES
    starts = [b * group + p for b in range(CONV_GROUPS) for p in range(CONV_STRIDE)]
    for s in range(n_slab):
        lanes = slice(s * LANES, (s + 1) * LANES)
        bias = jnp.broadcast_to(bdw_ref[:, lanes], (SUBLANES, LANES))
        for r0 in range(0, tm, CONV_GROUPS * group):
            accs = [bias for _ in starts]
            for k in range(k_conv):
                wk = jnp.broadcast_to(wdw_ref[k:k + 1, lanes], (SUBLANES, LANES))
                for q, st in enumerate(starts):
                    rows = pl.ds(r0 + st + shift + k, SUBLANES, stride=CONV_STRIDE)
                    accs[q] = accs[q] + wk * gext_ref[s, rows, :]
            for q, st in enumerate(starts):
                cout_ref[s, pl.ds(r0 + st, SUBLANES, stride=CONV_STRIDE), :] = accs[q]
        c_ref[:, lanes] = cout_ref[s]


def _proj_prompt_body(x_ref, gain_ref, w_ref, wdw_ref, bdw_ref,
                      u_ref, gv_ref, c_ref, ga_ref, gb_ref, tail_ref,
                      h_ref, raw_even_ref, raw_odd_ref, gext_ref, cout_ref, carry_ref,
                      *, n_steps, nblk, tiles_per_seq):
    s = pl.program_id(0)
    tm = x_ref.shape[0]
    n_slab = gext_ref.shape[0]
    j = jnp.minimum(s, n_steps - 1) % nblk
    prev = jnp.maximum(s - 1, 0)
    prev_i = prev // nblk
    prev_j = prev % nblk

    @pl.when(jnp.logical_and(j == 0, s < n_steps))
    def _():
        for r in range(0, tm, ROW_CHUNK):
            rows = slice(r, r + ROW_CHUNK)
            h_ref[rows, :] = _rms(x_ref[rows, :], gain_ref[...]).astype(BF16)

    @pl.when(s == 0)
    def _():
        raw_odd_ref[...] = jnp.zeros_like(raw_odd_ref)

    @pl.when(jnp.logical_and(prev_j == 0, prev_i % tiles_per_seq == 0))
    def _():
        carry_ref[...] = jnp.zeros_like(carry_ref)

    def step(raw_w, raw_r):
        g = _section_outputs(lambda sec: raw_r[sec], u_ref, gv_ref, ga_ref, gb_ref)
        gext_ref[:, 0:CARRY_ROWS, :] = carry_ref[prev_j]
        for sl in range(n_slab):
            gext_ref[sl, CARRY_ROWS:CARRY_ROWS + tm, :] = g[:, sl * LANES:(sl + 1) * LANES]
            carry_ref[prev_j, sl] = g[tm - CARRY_ROWS:, sl * LANES:(sl + 1) * LANES]
        tail_ref[0] = g[tm - CARRY_ROWS:, :]
        _causal_conv(gext_ref, cout_ref, wdw_ref, bdw_ref, c_ref)

        h = h_ref[...]
        for sec in range(N_SEC):
            raw_w[sec] = _dot(h, w_ref[0, sec])

    @pl.when(s % 2 == 0)
    def _():
        step(raw_even_ref, raw_odd_ref)

    @pl.when(s % 2 == 1)
    def _():
        step(raw_odd_ref, raw_even_ref)


def _proj_sample_body(x_ref, gain_ref, w_ref, wdw_ref, bdw_ref, cache_ref,
                      u_ref, gv_ref, c_ref, ga_ref, gb_ref, g_ref,
                      h_ref, *, n_seq):
    j = pl.program_id(1)
    k_conv = wdw_ref.shape[0]
    n_prev = k_conv - 1
    t_new = x_ref.shape[0] // n_seq

    @pl.when(j == 0)
    def _():
        h_ref[...] = _rms(x_ref[...], gain_ref[...]).astype(BF16)

    h = h_ref[...]
    g_ref[...] = _section_outputs(lambda sec: _dot(h, w_ref[0, sec]),
                                  u_ref, gv_ref, ga_ref, gb_ref)

    bias = bdw_ref[...]
    for t in range(t_new):
        acc = jnp.broadcast_to(bias, (n_seq, bias.shape[1]))
        for k in range(k_conv):
            m = t + k
            if m < n_prev:
                slab = cache_ref[m]
            else:
                slab = g_ref[(m - n_prev) * n_seq:(m - n_prev + 1) * n_seq, :]
            acc = acc + wdw_ref[k:k + 1, :] * slab
        c_ref[t * n_seq:(t + 1) * n_seq, :] = acc


def _proj_out_shapes(m, dsec):
    return [
        jax.ShapeDtypeStruct((m, dsec), BF16),
        jax.ShapeDtypeStruct((m, dsec), F32),
        jax.ShapeDtypeStruct((m, dsec), F32),
        jax.ShapeDtypeStruct((m, dsec), BF16),
        jax.ShapeDtypeStruct((m, dsec), BF16),
    ]


def _proj_prompt(x, gain, w_in_tiles, w_dw, b_dw, *, tm, seq_len):
    m, d = x.shape
    nblk, _, _, tn = w_in_tiles.shape
    dsec = nblk * tn
    n_slab = tn // LANES
    k_conv = w_dw.shape[0]
    assert m % tm == 0 and tn % LANES == 0
    assert seq_len % tm == 0 and tm >= CARRY_ROWS
    assert tm % (CONV_GROUPS * CONV_STRIDE * SUBLANES) == 0
    n_steps = (m // tm) * nblk
    cur = lambda s: jnp.minimum(s, n_steps - 1)
    prev = lambda s: jnp.maximum(s - 1, 0)
    out_tile = lambda s: (prev(s) // nblk, prev(s) % nblk)
    return pl.pallas_call(
        functools.partial(_proj_prompt_body, n_steps=n_steps, nblk=nblk,
                          tiles_per_seq=seq_len // tm),
        grid=(n_steps + 1,),
        in_specs=[
            pl.BlockSpec((tm, d), lambda s: (cur(s) // nblk, 0)),
            pl.BlockSpec((1, d), lambda s: (0, 0)),
            pl.BlockSpec((1, N_SEC, d, tn), lambda s: (cur(s) % nblk, 0, 0, 0)),
            pl.BlockSpec((k_conv, tn), lambda s: (0, prev(s) % nblk)),
            pl.BlockSpec((1, tn), lambda s: (0, prev(s) % nblk)),
        ],
        out_specs=[pl.BlockSpec((tm, tn), out_tile) for _ in range(5)]
        + [pl.BlockSpec((1, CARRY_ROWS, tn), lambda s: (prev(s) // nblk, 0, prev(s) % nblk))],
        out_shape=_proj_out_shapes(m, dsec)
        + [jax.ShapeDtypeStruct((m // tm, CARRY_ROWS, dsec), F32)],
        scratch_shapes=[
            pltpu.VMEM((tm, d), BF16),
            pltpu.VMEM((N_SEC, tm, tn), F32),
            pltpu.VMEM((N_SEC, tm, tn), F32),
            pltpu.VMEM((n_slab, CARRY_ROWS + tm, LANES), F32),
            pltpu.VMEM((n_slab, tm, LANES), F32),
            pltpu.VMEM((nblk, n_slab, CARRY_ROWS, LANES), F32),
        ],
        compiler_params=pltpu.CompilerParams(
            dimension_semantics=("arbitrary",), vmem_limit_bytes=VMEM_LIMIT_BYTES,
        ),
        name="proj_prompt",
    )(x, gain, w_in_tiles, w_dw, b_dw)


def _proj_sample(x, gain, w_in_tiles, w_dw, b_dw, cache_t):
    m, d = x.shape
    nblk, _, _, tn = w_in_tiles.shape
    dsec = nblk * tn
    k_conv = w_dw.shape[0]
    n_seq = cache_t.shape[1]
    assert m % n_seq == 0
    tile = lambda i, j: (i, j)
    return pl.pallas_call(
        functools.partial(_proj_sample_body, n_seq=n_seq),
        grid=(1, nblk),
        in_specs=[
            pl.BlockSpec((m, d), lambda i, j: (i, 0)),
            pl.BlockSpec((1, d), lambda i, j: (0, 0)),
            pl.BlockSpec((1, N_SEC, d, tn), lambda i, j: (j, 0, 0, 0)),
            pl.BlockSpec((k_conv, tn), lambda i, j: (0, j)),
            pl.BlockSpec((1, tn), lambda i, j: (0, j)),
            pl.BlockSpec((k_conv - 1, n_seq, tn), lambda i, j: (0, 0, j)),
        ],
        out_specs=[pl.BlockSpec((m, tn), tile) for _ in range(6)],
        out_shape=_proj_out_shapes(m, dsec) + [jax.ShapeDtypeStruct((m, dsec), F32)],
        scratch_shapes=[pltpu.VMEM((m, d), BF16)],
        compiler_params=_params(),
        name="proj_sample",
    )(x, gain, w_in_tiles, w_dw, b_dw, cache_t)


def _mixa_prompt_body(u_ref, gv_ref, c_ref, ws_ref, bs_ref, lnvg_ref, lnvb_ref, lncg_ref, lncb_ref,
                      a_ref, cc_ref, vst_ref, vb_ref):
    tr = u_ref.shape[0]
    n_head, chunk, _ = ws_ref.shape
    hd = u_ref.shape[1] // n_head
    for r in range(0, tr, ROW_CHUNK):
        rows = slice(r, r + ROW_CHUNK)
        v = _layer_norm(gv_ref[rows, :], lnvg_ref[...], lnvb_ref[...])
        if r >= tr - chunk:
            vst_ref[0, r - (tr - chunk):r - (tr - chunk) + ROW_CHUNK, :] = v
        vb_ref[rows, :] = v.astype(BF16)
        cc_ref[rows, :] = jax.nn.silu(
            _layer_norm(c_ref[rows, :], lncg_ref[...], lncb_ref[...])).astype(BF16)
    row = lax.broadcasted_iota(jnp.int32, (chunk, chunk), 0)
    col = lax.broadcasted_iota(jnp.int32, (chunk, chunk), 1)
    causal = (row >= col).astype(F32)
    for h in range(n_head):
        w = (ws_ref[h] * causal).astype(BF16)
        cols = slice(h * hd, (h + 1) * hd)
        for c in range(tr // chunk):
            rows = slice(c * chunk, (c + 1) * chunk)
            s = _dot(w, vb_ref[rows, cols]) + bs_ref[:, cols]
            a_ref[rows, cols] = (u_ref[rows, cols].astype(F32) * s).astype(BF16)


def _mixa_sample_body(u_ref, gv_ref, c_ref, coef_ref, bs_ref, lnvg_ref, lnvb_ref, lncg_ref,
                      lncb_ref, a_ref, cc_ref, v_ref, *, n_seq):
    t_new = u_ref.shape[0] // n_seq
    v_ref[...] = _layer_norm(gv_ref[...], lnvg_ref[...], lnvb_ref[...])
    for i in range(t_new):
        rows_i = slice(i * n_seq, (i + 1) * n_seq)
        s = jnp.broadcast_to(bs_ref[i:i + 1, :], (n_seq, bs_ref.shape[1]))
        for j in range(i + 1):
            s = s + coef_ref[i, j:j + 1, :] * v_ref[j * n_seq:(j + 1) * n_seq, :]
        a_ref[rows_i, :] = (u_ref[rows_i, :].astype(F32) * s).astype(BF16)
    cc_ref[...] = jax.nn.silu(_layer_norm(c_ref[...], lncg_ref[...], lncb_ref[...])).astype(BF16)


def _mixa_prompt(u, gv, c, w_s, bs_slab, ln_v_g, ln_v_b, ln_c_g, ln_c_b, *, tr, seq_len):
    m, d = u.shape
    n_head, chunk, _ = w_s.shape
    assert seq_len % tr == 0 and tr % chunk == 0 and chunk % ROW_CHUNK == 0
    tiles_per_seq = seq_len // tr
    row = lambda i: (i, 0)
    const2 = lambda i: (0, 0)
    return pl.pallas_call(
        _mixa_prompt_body,
        grid=(m // tr,),
        in_specs=[
            pl.BlockSpec((tr, d), row),
            pl.BlockSpec((tr, d), row),
            pl.BlockSpec((tr, d), row),
            pl.BlockSpec((n_head, chunk, chunk), lambda i: (0, 0, 0)),
            pl.BlockSpec((chunk, d), const2),
            pl.BlockSpec((1, d), const2),
            pl.BlockSpec((1, d), const2),
            pl.BlockSpec((1, d), const2),
            pl.BlockSpec((1, d), const2),
        ],
        out_specs=[
            pl.BlockSpec((tr, d), row),
            pl.BlockSpec((tr, d), row),
            pl.BlockSpec((1, chunk, d), lambda i: (i // tiles_per_seq, 0, 0)),
        ],
        out_shape=[
            jax.ShapeDtypeStruct((m, d), BF16),
            jax.ShapeDtypeStruct((m, d), BF16),
            jax.ShapeDtypeStruct((m // seq_len, chunk, d), F32),
        ],
        scratch_shapes=[pltpu.VMEM((tr, d), BF16)],
        compiler_params=pltpu.CompilerParams(dimension_semantics=("arbitrary",),
                                             vmem_limit_bytes=VMEM_LIMIT_BYTES),
        name="mixa_prompt",
    )(u, gv, c, w_s, bs_slab, ln_v_g, ln_v_b, ln_c_g, ln_c_b)


def _mixa_sample(u, gv, c, coef, bs_rows, ln_v_g, ln_v_b, ln_c_g, ln_c_b, *, n_seq):
    m, d = u.shape
    t_new = m // n_seq
    whole = lambda i: (0, 0)
    return pl.pallas_call(
        functools.partial(_mixa_sample_body, n_seq=n_seq),
        grid=(1,),
        in_specs=[
            pl.BlockSpec((m, d), whole),
            pl.BlockSpec((m, d), whole),
            pl.BlockSpec((m, d), whole),
            pl.BlockSpec((t_new, t_new, d), lambda i: (0, 0, 0)),
            pl.BlockSpec((t_new, d), whole),
            pl.BlockSpec((1, d), whole),
            pl.BlockSpec((1, d), whole),
            pl.BlockSpec((1, d), whole),
            pl.BlockSpec((1, d), whole),
        ],
        out_specs=[pl.BlockSpec((m, d), whole)] * 3,
        out_shape=[
            jax.ShapeDtypeStruct((m, d), BF16),
            jax.ShapeDtypeStruct((m, d), BF16),
            jax.ShapeDtypeStruct((m, d), F32),
        ],
        compiler_params=pltpu.CompilerParams(dimension_semantics=("arbitrary",),
                                             vmem_limit_bytes=VMEM_LIMIT_BYTES),
        name="mixa_sample",
    )(u, gv, c, coef, bs_rows, ln_v_g, ln_v_b, ln_c_g, ln_c_b)


def _mixb_body(x_ref, a_ref, cc_ref, ga_ref, gb_ref, wab_ref, wout_ref, o_ref):
    j = pl.program_id(1)

    @pl.when(j == 0)
    def _():
        o_ref[...] = jnp.zeros_like(o_ref)

    ya = _dot(a_ref[...], wab_ref[0, 0])
    yb = _dot(cc_ref[...], wab_ref[0, 1])
    merged = (ga_ref[...].astype(F32) * ya + gb_ref[...].astype(F32) * yb).astype(BF16)
    o_ref[...] += _dot(merged, wout_ref[...])

    @pl.when(j == pl.num_programs(1) - 1)
    def _():
        for r in range(0, x_ref.shape[0], ROW_CHUNK):
            rows = slice(r, r + ROW_CHUNK)
            o_ref[rows, :] = x_ref[rows, :] + o_ref[rows, :]


def _mixb(x, a, cc, ga, gb, w_ab_tiles, w_out, *, tr):
    m, d = x.shape
    nj, _, d_branch, tj = w_ab_tiles.shape
    assert m % tr == 0 and w_out.shape[0] == nj * tj
    row = lambda i, j: (i, 0)
    return pl.pallas_call(
        _mixb_body,
        grid=(m // tr, nj),
        in_specs=[
            pl.BlockSpec((tr, d), row),
            pl.BlockSpec((tr, a.shape[1]), row),
            pl.BlockSpec((tr, cc.shape[1]), row),
            pl.BlockSpec((tr, tj), lambda i, j: (i, j)),
            pl.BlockSpec((tr, tj), lambda i, j: (i, j)),
            pl.BlockSpec((1, 2, d_branch, tj), lambda i, j: (j, 0, 0, 0)),
            pl.BlockSpec((tj, d), lambda i, j: (j, 0)),
        ],
        out_specs=pl.BlockSpec((tr, d), row),
        out_shape=jax.ShapeDtypeStruct((m, d), F32),
        compiler_params=_params(),
        name="mixb",
    )(x, a, cc, ga, gb, w_ab_tiles, w_out)


def kernel(x_prompt, x_sample, cache_conv, ffn1_norm, ffn1_w_gu, ffn1_w_down, mix_norm, w_in, w_s, b_s, ln_v_g, ln_v_b, w_pa, w_dw, b_dw, ln_c_g, ln_c_b, w_pb, w_out, ffn2_norm, ffn2_w_gu, ffn2_w_down, final_norm):
    n_p, t_p, d = x_prompt.shape
    n_s, t_s, _ = x_sample.shape
    depth, n_head, chunk, _ = w_s.shape
    d_a = ln_v_g.shape[1]
    hd = d_a // n_head
    k_conv = w_dw.shape[1]
    assert t_s <= chunk and k_conv - 1 <= CARRY_ROWS and n_s % SUBLANES == 0
    last_len = t_p - chunk * ((t_p - 1) // chunk)
    assert last_len == chunk, "prompt length must be a whole number of chunks"

    xp = x_prompt.reshape(n_p * t_p, d)
    xs = jnp.swapaxes(x_sample, 0, 1).reshape(t_s * n_s, d)
    m_s = t_s * n_s
    row2 = lambda v: v.reshape(1, -1)
    fgain = row2(final_norm)

    tiles = _tile_sizes()
    vp_list, vs_list, cp_list, cs_list = [], [], [], []
    for l in range(depth):
        last = l == depth - 1
        w1gu, w1d = _tile_columns(ffn1_w_gu[l], 2, tiles.ffn_hidden), ffn1_w_down[l].astype(BF16)
        w2gu, w2d = _tile_columns(ffn2_w_gu[l], 2, tiles.ffn_hidden), ffn2_w_down[l].astype(BF16)
        win = _tile_columns(w_in[l], N_SEC, tiles.proj_cols)
        wab = _tile_columns(jnp.concatenate([w_pa[l], w_pb[l]], axis=1), 2, tiles.mix_cols)
        wout = w_out[l].astype(BF16)
        bdw = row2(b_dw[l])
        lnvg, lnvb, lncg, lncb = row2(ln_v_g[l]), row2(ln_v_b[l]), row2(ln_c_g[l]), row2(ln_c_b[l])
        bs_slab = jnp.repeat(b_s[l].T, hd, axis=1)
        coef = jnp.repeat(jnp.transpose(w_s[l][:, :t_s, :t_s], (1, 2, 0)), hd, axis=2)
        cache_t = jnp.swapaxes(cache_conv[l], 0, 1)

        xp = _ffn(xp, row2(ffn1_norm[l]), w1gu, w1d, fgain, tm=tiles.ffn_rows, final_norm=False)
        u, gv, c, ga, gb, tail = _proj_prompt(xp, row2(mix_norm[l]), win, w_dw[l], bdw,
                                              tm=tiles.proj_rows, seq_len=t_p)
        tiles_per_seq = t_p // tiles.proj_rows
        tail = tail[tiles_per_seq - 1::tiles_per_seq]
        a, cc, v_p = _mixa_prompt(u, gv, c, w_s[l], bs_slab, lnvg, lnvb, lncg, lncb,
                                  tr=tiles.mixa_rows, seq_len=t_p)
        xp = _mixb(xp, a, cc, ga, gb, wab, wout, tr=tiles.mix_rows)
        xp = _ffn(xp, row2(ffn2_norm[l]), w2gu, w2d, fgain, tm=tiles.ffn_rows, final_norm=last)
        vp_list.append(v_p)
        cp_list.append(tail[:, CARRY_ROWS - (k_conv - 1):, :])

        xs = _ffn(xs, row2(ffn1_norm[l]), w1gu, w1d, fgain, tm=m_s, final_norm=False)
        u, gv, c, ga, gb, g_s = _proj_sample(xs, row2(mix_norm[l]), win, w_dw[l], bdw, cache_t)
        a, cc, v_s = _mixa_sample(u, gv, c, coef, bs_slab[:t_s], lnvg, lnvb, lncg, lncb, n_seq=n_s)
        xs = _mixb(xs, a, cc, ga, gb, wab, wout, tr=m_s)
        xs = _ffn(xs, row2(ffn2_norm[l]), w2gu, w2d, fgain, tm=m_s, final_norm=last)
        vs_list.append(jnp.swapaxes(v_s.reshape(t_s, n_s, d_a), 0, 1))
        g_seq = jnp.swapaxes(g_s.reshape(t_s, n_s, -1), 0, 1)
        cs_list.append(jnp.concatenate([cache_conv[l], g_seq], axis=1)[:, -(k_conv - 1):])

    y_prompt = xp.reshape(n_p, t_p, d)
    y_sample = jnp.swapaxes(xs.reshape(t_s, n_s, d), 0, 1)
    return (y_prompt, y_sample, jnp.stack(vp_list), jnp.stack(vs_list),
            jnp.stack(cp_list), jnp.stack(cs_list))
```

```python
import functools
from typing import NamedTuple

import jax
import jax.numpy as jnp
from jax import lax
from jax.experimental import pallas as pl
from jax.experimental.pallas import tpu as pltpu

F32 = jnp.float32
BF16 = jnp.bfloat16
EPS = 1e-6

V7X_VMEM_BYTES = 64 * 1024 * 1024
VMEM_LIMIT_BYTES = V7X_VMEM_BYTES - 4 * 1024 * 1024
SUBLANES = 8
LANES = 128
CONV_STRIDE = 2
CONV_GROUPS = 4
CARRY_ROWS = 32
ROW_CHUNK = 64

SEC_UA, SEC_VA, SEC_GLU_A, SEC_GLU_B, SEC_GATE_A, SEC_GATE_B = range(6)
N_SEC = 6


class _Tiles(NamedTuple):
    ffn_rows: int
    ffn_hidden: int
    ffn_hidden_emit: int
    proj_rows: int
    proj_cols: int
    mixa_rows: int
    mix_rows: int
    mix_cols: int


def _tile_sizes():
    return _Tiles(ffn_rows=1024, ffn_hidden=512, ffn_hidden_emit=256, proj_rows=512, proj_cols=256,
                  mixa_rows=256, mix_rows=1024, mix_cols=256)


def _params(n_axes=2):
    return pltpu.CompilerParams(
        dimension_semantics=("arbitrary",) * n_axes,
        vmem_limit_bytes=VMEM_LIMIT_BYTES,
    )


def _rms(x, gain):
    ms = jnp.mean(x * x, axis=-1, keepdims=True)
    return x * lax.rsqrt(ms + EPS) * gain


def _layer_norm(x, gain, bias):
    mu = jnp.mean(x, axis=-1, keepdims=True)
    xc = x - mu
    var = jnp.mean(xc * xc, axis=-1, keepdims=True)
    return xc * lax.rsqrt(var + EPS) * gain + bias


def _dot(a, b):
    return jnp.dot(a, b, preferred_element_type=F32)


def _pack_rows(x):
    return pltpu.bitcast(x.astype(BF16), jnp.uint32)


def _unpack_rows(packed):
    return pltpu.bitcast(packed, BF16)


def _for_each_row_chunk(n_rows, fn):
    half = ROW_CHUNK // 2

    def body(c, carry):
        fn(pl.ds(pl.multiple_of(c * ROW_CHUNK, ROW_CHUNK), ROW_CHUNK),
           pl.ds(pl.multiple_of(c * half, half), half))
        return carry

    lax.fori_loop(0, n_rows // ROW_CHUNK, body, 0)


def _ffn_body(x_ref, gain_ref, wg_ref, wu_ref, wd_ref, post_gain_ref, o_ref, *rest,
              final_norm, emit_next_norm):
    if emit_next_norm:
        next_ref, xn_ref = rest
    else:
        (xn_ref,) = rest
    k = pl.program_id(1)
    tm = x_ref.shape[0]

    @pl.when(k == 0)
    def _():
        def norm_chunk(rows, _):
            xn_ref[rows, :] = _rms(x_ref[rows, :], gain_ref[...]).astype(BF16)

        _for_each_row_chunk(tm, norm_chunk)
        o_ref[...] = jnp.zeros_like(o_ref)

    xn = xn_ref[...]
    gate = _dot(xn, wg_ref[...])
    up = _dot(xn, wu_ref[...])
    act = (jax.nn.silu(gate) * up).astype(BF16)
    o_ref[...] += _dot(act, wd_ref[...])

    @pl.when(k == pl.num_programs(1) - 1)
    def _():
        def finish_chunk(rows, packed_rows):
            y = x_ref[rows, :] + 0.5 * o_ref[rows, :]
            if final_norm:
                y = _rms(y, post_gain_ref[...])
            o_ref[rows, :] = y
            if emit_next_norm:
                next_ref[packed_rows, :] = _pack_rows(_rms(y, post_gain_ref[...]))

        _for_each_row_chunk(tm, finish_chunk)


def _ffn(x, gain, w_gu, w_down, post_gain, *, tm, tf, final_norm=False, emit_next_norm=False):
    m, d = x.shape
    f = w_down.shape[0]
    nf = f // tf
    assert m % tm == 0 and f % tf == 0 and tm % ROW_CHUNK == 0
    assert not (final_norm and emit_next_norm)
    rows = lambda i, k: (i, 0)
    out_specs = [pl.BlockSpec((tm, d), rows)]
    out_shape = [jax.ShapeDtypeStruct((m, d), F32)]
    if emit_next_norm:
        out_specs.append(pl.BlockSpec((tm // 2, d), rows))
        out_shape.append(jax.ShapeDtypeStruct((m // 2, d), jnp.uint32))
    out = pl.pallas_call(
        functools.partial(_ffn_body, final_norm=final_norm, emit_next_norm=emit_next_norm),
        grid=(m // tm, nf),
        in_specs=[
            pl.BlockSpec((tm, d), rows),
            pl.BlockSpec((1, d), lambda i, k: (0, 0)),
            pl.BlockSpec((d, tf), lambda i, k: (0, k)),
            pl.BlockSpec((d, tf), lambda i, k: (0, k + nf)),
            pl.BlockSpec((tf, d), lambda i, k: (k, 0)),
            pl.BlockSpec((1, d), lambda i, k: (0, 0)),
        ],
        out_specs=out_specs,
        out_shape=out_shape,
        scratch_shapes=[pltpu.VMEM((tm, d), BF16)],
        compiler_params=_params(),
        name="ffn",
    )(x, gain, w_gu, w_gu, w_down, post_gain)
    return out if emit_next_norm else out[0]


def _section_outputs(pre, u_ref, gv_ref, ga_ref, gb_ref):
    u_ref[...] = jax.nn.gelu(pre(SEC_UA)).astype(BF16)
    gv_ref[...] = jax.nn.gelu(pre(SEC_VA))
    ga_ref[...] = jax.nn.sigmoid(pre(SEC_GATE_A)).astype(BF16)
    gb_ref[...] = jax.nn.sigmoid(pre(SEC_GATE_B)).astype(BF16)
    return pre(SEC_GLU_A) * jax.nn.sigmoid(pre(SEC_GLU_B))


def _causal_conv(gext_ref, cout_ref, wdw_ref, bdw_ref, c_ref):
    n_slab, tm, _ = cout_ref.shape
    k_conv = wdw_ref.shape[0]
    shift = CARRY_ROWS - (k_conv - 1)
    group = CONV_STRIDE * SUBLANES
    starts = [b * group + p for b in range(CONV_GROUPS) for p in range(CONV_STRIDE)]
    for s in range(n_slab):
        lanes = slice(s * LANES, (s + 1) * LANES)
        bias = jnp.broadcast_to(bdw_ref[:, lanes], (SUBLANES, LANES))
        for r0 in range(0, tm, CONV_GROUPS * group):
            accs = [bias for _ in starts]
            for k in range(k_conv):
                wk = jnp.broadcast_to(wdw_ref[k:k + 1, lanes], (SUBLANES, LANES))
                for q, st in enumerate(starts):
                    rows = pl.ds(r0 + st + shift + k, SUBLANES, stride=CONV_STRIDE)
                    accs[q] = accs[q] + wk * gext_ref[s, rows, :]
            for q, st in enumerate(starts):
                cout_ref[s, pl.ds(r0 + st, SUBLANES, stride=CONV_STRIDE), :] = accs[q]
        c_ref[:, lanes] = cout_ref[s]


def _proj_prompt_body(h_ref, w0, w1, w2, w3, w4, w5, wdw_ref, bdw_ref,
                      u_ref, gv_ref, c_ref, ga_ref, gb_ref, tail_ref,
                      raw_even_ref, raw_odd_ref, gext_ref, cout_ref, carry_ref,
                      *, n_steps, tiles_per_seq):
    s = pl.program_id(0)
    n_slab, _, _ = cout_ref.shape
    tm = cout_ref.shape[1]
    w_refs = (w0, w1, w2, w3, w4, w5)
    prev_sub = jnp.maximum(s - 1, 0) % tiles_per_seq

    @pl.when(s == 0)
    def _():
        raw_odd_ref[...] = jnp.zeros_like(raw_odd_ref)

    @pl.when(prev_sub == 0)
    def _():
        carry_ref[...] = jnp.zeros_like(carry_ref)

    def step(raw_w, raw_r):
        h = _unpack_rows(h_ref[...])
        g = _section_outputs(lambda sec: raw_r[sec], u_ref, gv_ref, ga_ref, gb_ref)
        gext_ref[:, 0:CARRY_ROWS, :] = carry_ref[...]
        for sl in range(n_slab):
            gext_ref[sl, CARRY_ROWS:CARRY_ROWS + tm, :] = g[:, sl * LANES:(sl + 1) * LANES]
            carry_ref[sl] = g[tm - CARRY_ROWS:, sl * LANES:(sl + 1) * LANES]
        tail_ref[0] = g[tm - CARRY_ROWS:, :]
        _causal_conv(gext_ref, cout_ref, wdw_ref, bdw_ref, c_ref)

        for sec in range(N_SEC):
            raw_w[sec] = _dot(h, w_refs[sec][...])

    @pl.when(s % 2 == 0)
    def _():
        step(raw_even_ref, raw_odd_ref)

    @pl.when(s % 2 == 1)
    def _():
        step(raw_odd_ref, raw_even_ref)


def _proj_sample_body(h_ref, w0, w1, w2, w3, w4, w5, wdw_ref, bdw_ref, cache_ref,
                      u_ref, gv_ref, c_ref, ga_ref, gb_ref, g_ref, *, n_seq):
    w_refs = (w0, w1, w2, w3, w4, w5)
    k_conv = wdw_ref.shape[0]
    n_prev = k_conv - 1
    t_new = 2 * h_ref.shape[0] // n_seq

    h = _unpack_rows(h_ref[...])
    g_ref[...] = _section_outputs(lambda sec: _dot(h, w_refs[sec][...]),
                                  u_ref, gv_ref, ga_ref, gb_ref)

    bias = bdw_ref[...]
    for t in range(t_new):
        acc = jnp.broadcast_to(bias, (n_seq, bias.shape[1]))
        for k in range(k_conv):
            m = t + k
            if m < n_prev:
                slab = cache_ref[m]
            else:
                slab = g_ref[(m - n_prev) * n_seq:(m - n_prev + 1) * n_seq, :]
            acc = acc + wdw_ref[k:k + 1, :] * slab
        c_ref[t * n_seq:(t + 1) * n_seq, :] = acc


def _proj_out_shapes(m, dsec):
    return [
        jax.ShapeDtypeStruct((m, dsec), BF16),
        jax.ShapeDtypeStruct((m, dsec), F32),
        jax.ShapeDtypeStruct((m, dsec), F32),
        jax.ShapeDtypeStruct((m, dsec), BF16),
        jax.ShapeDtypeStruct((m, dsec), BF16),
    ]


def _proj_prompt(h, w_in, w_dw, b_dw, *, tm, tn, seq_len):
    m, d = 2 * h.shape[0], h.shape[1]
    dsec = w_in.shape[1] // N_SEC
    nblk = dsec // tn
    n_slab = tn // LANES
    k_conv = w_dw.shape[0]
    assert m % tm == 0 and dsec % tn == 0 and tn % LANES == 0
    assert seq_len % tm == 0 and tm >= CARRY_ROWS
    assert tm % (CONV_GROUPS * CONV_STRIDE * SUBLANES) == 0
    tps = seq_len // tm
    n_steps = (m // tm) * nblk
    seq_of = lambda t: t // (nblk * tps)
    col_of = lambda t: (t // tps) % nblk
    row_tile_of = lambda t: seq_of(t) * tps + t % tps
    cur = lambda s: jnp.minimum(s, n_steps - 1)
    prev = lambda s: jnp.maximum(s - 1, 0)
    w_specs = [pl.BlockSpec((d, tn), lambda s, sec=sec: (0, sec * nblk + col_of(cur(s))))
               for sec in range(N_SEC)]
    out_tile = lambda s: (row_tile_of(prev(s)), col_of(prev(s)))
    return pl.pallas_call(
        functools.partial(_proj_prompt_body, n_steps=n_steps, tiles_per_seq=tps),
        grid=(n_steps + 1,),
        in_specs=[
            pl.BlockSpec((tm // 2, d), lambda s: (row_tile_of(cur(s)), 0)),
            *w_specs,
            pl.BlockSpec((k_conv, tn), lambda s: (0, col_of(prev(s)))),
            pl.BlockSpec((1, tn), lambda s: (0, col_of(prev(s)))),
        ],
        out_specs=[pl.BlockSpec((tm, tn), out_tile) for _ in range(5)]
        + [pl.BlockSpec((1, CARRY_ROWS, tn),
                        lambda s: (row_tile_of(prev(s)), 0, col_of(prev(s))))],
        out_shape=_proj_out_shapes(m, dsec)
        + [jax.ShapeDtypeStruct((m // tm, CARRY_ROWS, dsec), F32)],
        scratch_shapes=[
            pltpu.VMEM((N_SEC, tm, tn), F32),
            pltpu.VMEM((N_SEC, tm, tn), F32),
            pltpu.VMEM((n_slab, CARRY_ROWS + tm, LANES), F32),
            pltpu.VMEM((n_slab, tm, LANES), F32),
            pltpu.VMEM((n_slab, CARRY_ROWS, LANES), F32),
        ],
        compiler_params=_params(1),
        name="proj_prompt",
    )(h, *([w_in] * N_SEC), w_dw, b_dw)


def _proj_sample(h, w_in, w_dw, b_dw, cache_t, *, tn):
    m, d = 2 * h.shape[0], h.shape[1]
    dsec = w_in.shape[1] // N_SEC
    nblk = dsec // tn
    k_conv = w_dw.shape[0]
    n_seq = cache_t.shape[1]
    assert dsec % tn == 0 and m % n_seq == 0
    w_specs = [pl.BlockSpec((d, tn), lambda j, sec=sec: (0, sec * nblk + j))
               for sec in range(N_SEC)]
    return pl.pallas_call(
        functools.partial(_proj_sample_body, n_seq=n_seq),
        grid=(nblk,),
        in_specs=[
            pl.BlockSpec((m // 2, d), lambda j: (0, 0)),
            *w_specs,
            pl.BlockSpec((k_conv, tn), lambda j: (0, j)),
            pl.BlockSpec((1, tn), lambda j: (0, j)),
            pl.BlockSpec((k_conv - 1, n_seq, tn), lambda j: (0, 0, j)),
        ],
        out_specs=[pl.BlockSpec((m, tn), lambda j: (0, j)) for _ in range(6)],
        out_shape=_proj_out_shapes(m, dsec) + [jax.ShapeDtypeStruct((m, dsec), F32)],
        compiler_params=_params(1),
        name="proj_sample",
    )(h, *([w_in] * N_SEC), w_dw, b_dw, cache_t)


def _mixa_prompt_body(u_ref, gv_ref, c_ref, ws_ref, bs_ref, lnvg_ref, lnvb_ref, lncg_ref, lncb_ref,
                      a_ref, cc_ref, vst_ref, vb_ref):
    tr = u_ref.shape[0]
    n_head, chunk, _ = ws_ref.shape
    hd = u_ref.shape[1] // n_head
    for r in range(0, tr, ROW_CHUNK):
        rows = slice(r, r + ROW_CHUNK)
        v = _layer_norm(gv_ref[rows, :], lnvg_ref[...], lnvb_ref[...])
        if r >= tr - chunk:
            vst_ref[0, r - (tr - chunk):r - (tr - chunk) + ROW_CHUNK, :] = v
        vb_ref[rows, :] = v.astype(BF16)
        cc_ref[r // 2:(r + ROW_CHUNK) // 2, :] = _pack_rows(jax.nn.silu(
            _layer_norm(c_ref[rows, :], lncg_ref[...], lncb_ref[...])))
    row = lax.broadcasted_iota(jnp.int32, (chunk, chunk), 0)
    col = lax.broadcasted_iota(jnp.int32, (chunk, chunk), 1)
    causal = (row >= col).astype(F32)
    for h in range(n_head):
        w = (ws_ref[h] * causal).astype(BF16)
        cols = slice(h * hd, (h + 1) * hd)
        for c in range(tr // chunk):
            rows = slice(c * chunk, (c + 1) * chunk)
            s = _dot(w, vb_ref[rows, cols]) + bs_ref[:, cols]
            packed_rows = slice(c * chunk // 2, (c + 1) * chunk // 2)
            a_ref[packed_rows, cols] = _pack_rows(u_ref[rows, cols].astype(F32) * s)


def _mixa_sample_body(u_ref, gv_ref, c_ref, coef_ref, bs_ref, lnvg_ref, lnvb_ref, lncg_ref,
                      lncb_ref, a_ref, cc_ref, v_ref, *, n_seq):
    t_new = u_ref.shape[0] // n_seq
    v_ref[...] = _layer_norm(gv_ref[...], lnvg_ref[...], lnvb_ref[...])
    for i in range(t_new):
        rows_i = slice(i * n_seq, (i + 1) * n_seq)
        s = jnp.broadcast_to(bs_ref[i:i + 1, :], (n_seq, bs_ref.shape[1]))
        for j in range(i + 1):
            s = s + coef_ref[i, j:j + 1, :] * v_ref[j * n_seq:(j + 1) * n_seq, :]
        a_ref[i * n_seq // 2:(i + 1) * n_seq // 2, :] = _pack_rows(
            u_ref[rows_i, :].astype(F32) * s)
    cc_ref[...] = _pack_rows(
        jax.nn.silu(_layer_norm(c_ref[...], lncg_ref[...], lncb_ref[...])))


def _mixa_prompt(u, gv, c, w_s, bs_slab, ln_v_g, ln_v_b, ln_c_g, ln_c_b, *, tr, seq_len):
    m, d = u.shape
    n_head, chunk, _ = w_s.shape
    assert seq_len % tr == 0 and tr % chunk == 0 and chunk % ROW_CHUNK == 0
    tiles_per_seq = seq_len // tr
    row = lambda i: (i, 0)
    const2 = lambda i: (0, 0)
    return pl.pallas_call(
        _mixa_prompt_body,
        grid=(m // tr,),
        in_specs=[
            pl.BlockSpec((tr, d), row),
            pl.BlockSpec((tr, d), row),
            pl.BlockSpec((tr, d), row),
            pl.BlockSpec((n_head, chunk, chunk), lambda i: (0, 0, 0)),
            pl.BlockSpec((chunk, d), const2),
            pl.BlockSpec((1, d), const2),
            pl.BlockSpec((1, d), const2),
            pl.BlockSpec((1, d), const2),
            pl.BlockSpec((1, d), const2),
        ],
        out_specs=[
            pl.BlockSpec((tr // 2, d), row),
            pl.BlockSpec((tr // 2, d), row),
            pl.BlockSpec((1, chunk, d), lambda i: (i // tiles_per_seq, 0, 0)),
        ],
        out_shape=[
            jax.ShapeDtypeStruct((m // 2, d), jnp.uint32),
            jax.ShapeDtypeStruct((m // 2, d), jnp.uint32),
            jax.ShapeDtypeStruct((m // seq_len, chunk, d), F32),
        ],
        scratch_shapes=[pltpu.VMEM((tr, d), BF16)],
        compiler_params=_params(1),
        name="mixa_prompt",
    )(u, gv, c, w_s, bs_slab, ln_v_g, ln_v_b, ln_c_g, ln_c_b)


def _mixa_sample(u, gv, c, coef, bs_rows, ln_v_g, ln_v_b, ln_c_g, ln_c_b, *, n_seq):
    m, d = u.shape
    t_new = m // n_seq
    whole = lambda i: (0, 0)
    return pl.pallas_call(
        functools.partial(_mixa_sample_body, n_seq=n_seq),
        grid=(1,),
        in_specs=[
            pl.BlockSpec((m, d), whole),
            pl.BlockSpec((m, d), whole),
            pl.BlockSpec((m, d), whole),
            pl.BlockSpec((t_new, t_new, d), lambda i: (0, 0, 0)),
            pl.BlockSpec((t_new, d), whole),
            pl.BlockSpec((1, d), whole),
            pl.BlockSpec((1, d), whole),
            pl.BlockSpec((1, d), whole),
            pl.BlockSpec((1, d), whole),
        ],
        out_specs=[pl.BlockSpec((m // 2, d), whole), pl.BlockSpec((m // 2, d), whole),
                   pl.BlockSpec((m, d), whole)],
        out_shape=[
            jax.ShapeDtypeStruct((m // 2, d), jnp.uint32),
            jax.ShapeDtypeStruct((m // 2, d), jnp.uint32),
            jax.ShapeDtypeStruct((m, d), F32),
        ],
        compiler_params=_params(1),
        name="mixa_sample",
    )(u, gv, c, coef, bs_rows, ln_v_g, ln_v_b, ln_c_g, ln_c_b)


def _mixb_body(x_ref, a_ref, cc_ref, ga_ref, gb_ref, wpa_ref, wpb_ref, wout_ref, o_ref):
    j = pl.program_id(1)

    @pl.when(j == 0)
    def _():
        o_ref[...] = jnp.zeros_like(o_ref)

    ya = _dot(_unpack_rows(a_ref[...]), wpa_ref[...])
    yb = _dot(_unpack_rows(cc_ref[...]), wpb_ref[...])
    merged = (ga_ref[...].astype(F32) * ya + gb_ref[...].astype(F32) * yb).astype(BF16)
    o_ref[...] += _dot(merged, wout_ref[...])

    @pl.when(j == pl.num_programs(1) - 1)
    def _():
        def add_chunk(rows, _):
            o_ref[rows, :] = x_ref[rows, :] + o_ref[rows, :]

        _for_each_row_chunk(x_ref.shape[0], add_chunk)


def _mixb(x, a, cc, ga, gb, w_pa, w_pb, w_out, *, tr, tj):
    m, d = x.shape
    dm = w_pa.shape[1]
    assert m % tr == 0 and dm % tj == 0 and tr % ROW_CHUNK == 0
    row = lambda i, j: (i, 0)
    return pl.pallas_call(
        _mixb_body,
        grid=(m // tr, dm // tj),
        in_specs=[
            pl.BlockSpec((tr, d), row),
            pl.BlockSpec((tr // 2, a.shape[1]), row),
            pl.BlockSpec((tr // 2, cc.shape[1]), row),
            pl.BlockSpec((tr, tj), lambda i, j: (i, j)),
            pl.BlockSpec((tr, tj), lambda i, j: (i, j)),
            pl.BlockSpec((w_pa.shape[0], tj), lambda i, j: (0, j)),
            pl.BlockSpec((w_pb.shape[0], tj), lambda i, j: (0, j)),
            pl.BlockSpec((tj, d), lambda i, j: (j, 0)),
        ],
        out_specs=pl.BlockSpec((tr, d), row),
        out_shape=jax.ShapeDtypeStruct((m, d), F32),
        compiler_params=_params(),
        name="mixb",
    )(x, a, cc, ga, gb, w_pa, w_pb, w_out)


def kernel(x_prompt, x_sample, cache_conv, ffn1_norm, ffn1_w_gu, ffn1_w_down, mix_norm, w_in, w_s, b_s, ln_v_g, ln_v_b, w_pa, w_dw, b_dw, ln_c_g, ln_c_b, w_pb, w_out, ffn2_norm, ffn2_w_gu, ffn2_w_down, final_norm):
    n_p, t_p, d = x_prompt.shape
    n_s, t_s, _ = x_sample.shape
    depth, n_head, chunk, _ = w_s.shape
    d_a = ln_v_g.shape[1]
    hd = d_a // n_head
    k_conv = w_dw.shape[1]
    assert t_s <= chunk and k_conv - 1 <= CARRY_ROWS and n_s % (2 * SUBLANES) == 0
    last_len = t_p - chunk * ((t_p - 1) // chunk)
    assert last_len == chunk, "prompt length must be a whole number of chunks"

    xp = x_prompt.reshape(n_p * t_p, d)
    xs = jnp.swapaxes(x_sample, 0, 1).reshape(t_s * n_s, d)
    m_s = t_s * n_s
    row2 = lambda v: v.reshape(1, -1)
    fgain = row2(final_norm)

    tiles = _tile_sizes()
    tiles_per_seq = t_p // tiles.proj_rows
    vp_list, vs_list, cp_list, cs_list = [], [], [], []
    for l in range(depth):
        last = l == depth - 1
        w1gu, w1d = ffn1_w_gu[l].astype(BF16), ffn1_w_down[l].astype(BF16)
        w2gu, w2d = ffn2_w_gu[l].astype(BF16), ffn2_w_down[l].astype(BF16)
        win = w_in[l].astype(BF16)
        wpa, wpb, wout = w_pa[l].astype(BF16), w_pb[l].astype(BF16), w_out[l].astype(BF16)
        g1, gm, g2 = row2(ffn1_norm[l]), row2(mix_norm[l]), row2(ffn2_norm[l])
        bdw = row2(b_dw[l])
        lnvg, lnvb, lncg, lncb = row2(ln_v_g[l]), row2(ln_v_b[l]), row2(ln_c_g[l]), row2(ln_c_b[l])
        bs_slab = jnp.repeat(b_s[l].T, hd, axis=1)
        coef = jnp.repeat(jnp.transpose(w_s[l][:, :t_s, :t_s], (1, 2, 0)), hd, axis=2)
        cache_t = jnp.swapaxes(cache_conv[l], 0, 1)

        xp, hp = _ffn(xp, g1, w1gu, w1d, gm, tm=tiles.ffn_rows, tf=tiles.ffn_hidden_emit,
                      emit_next_norm=True)
        u, gv, c, ga, gb, tail = _proj_prompt(hp, win, w_dw[l], bdw, tm=tiles.proj_rows,
                                              tn=tiles.proj_cols, seq_len=t_p)
        tail = tail[tiles_per_seq - 1::tiles_per_seq]
        a, cc, v_p = _mixa_prompt(u, gv, c, w_s[l], bs_slab, lnvg, lnvb, lncg, lncb,
                                  tr=tiles.mixa_rows, seq_len=t_p)
        xp = _mixb(xp, a, cc, ga, gb, wpa, wpb, wout, tr=tiles.mix_rows, tj=tiles.mix_cols)
        xp = _ffn(xp, g2, w2gu, w2d, fgain, tm=tiles.ffn_rows, tf=tiles.ffn_hidden,
                  final_norm=last)
        vp_list.append(v_p)
        cp_list.append(tail[:, CARRY_ROWS - (k_conv - 1):, :])

        xs, hs = _ffn(xs, g1, w1gu, w1d, gm, tm=m_s, tf=tiles.ffn_hidden, emit_next_norm=True)
        u, gv, c, ga, gb, g_s = _proj_sample(hs, win, w_dw[l], bdw, cache_t, tn=tiles.proj_cols)
        a, cc, v_s = _mixa_sample(u, gv, c, coef, bs_slab[:t_s], lnvg, lnvb, lncg, lncb, n_seq=n_s)
        xs = _mixb(xs, a, cc, ga, gb, wpa, wpb, wout, tr=m_s, tj=tiles.mix_cols)
        xs = _ffn(xs, g2, w2gu, w2d, fgain, tm=m_s, tf=tiles.ffn_hidden, final_norm=last)
        vs_list.append(jnp.swapaxes(v_s.reshape(t_s, n_s, d_a), 0, 1))
        g_seq = jnp.swapaxes(g_s.reshape(t_s, n_s, -1), 0, 1)
        cs_list.append(jnp.concatenate([cache_conv[l], g_seq], axis=1)[:, -(k_conv - 1):])

    y_prompt = xp.reshape(n_p, t_p, d)
    y_sample = jnp.swapaxes(xs.reshape(t_s, n_s, d), 0, 1)
    return (y_prompt, y_sample, jnp.stack(vp_list), jnp.stack(vs_list),
            jnp.stack(cp_list), jnp.stack(cs_list))
```

```python
import functools
from typing import NamedTuple

import jax
import jax.numpy as jnp
from jax import lax
from jax.experimental import pallas as pl
from jax.experimental.pallas import tpu as pltpu

F32 = jnp.float32
BF16 = jnp.bfloat16
EPS = 1e-6

V7X_VMEM_BYTES = 64 * 1024 * 1024
VMEM_LIMIT_BYTES = V7X_VMEM_BYTES - 4 * 1024 * 1024
SUBLANES = 8
LANES = 128
CONV_STRIDE = 2
CONV_GROUPS = 4
CARRY_ROWS = 32
ROW_CHUNK = 64

SEC_UA, SEC_VA, SEC_GLU_A, SEC_GLU_B, SEC_GATE_A, SEC_GATE_B = range(6)
N_SEC = 6


class _Tiles(NamedTuple):
    ffn_rows: int
    ffn_hidden: int
    ffn_hidden_emit: int
    proj_rows: int
    proj_cols: int
    mixa_rows: int
    mix_rows: int
    mix_cols: int


def _tile_sizes():
    return _Tiles(ffn_rows=1024, ffn_hidden=512, ffn_hidden_emit=256, proj_rows=512, proj_cols=256,
                  mixa_rows=256, mix_rows=1024, mix_cols=256)


def _params(n_axes=2):
    return pltpu.CompilerParams(
        dimension_semantics=("arbitrary",) * n_axes,
        vmem_limit_bytes=VMEM_LIMIT_BYTES,
    )


def _rms(x, gain):
    ms = jnp.mean(x * x, axis=-1, keepdims=True)
    return x * lax.rsqrt(ms + EPS) * gain


def _layer_norm(x, gain, bias):
    mu = jnp.mean(x, axis=-1, keepdims=True)
    xc = x - mu
    var = jnp.mean(xc * xc, axis=-1, keepdims=True)
    return xc * lax.rsqrt(var + EPS) * gain + bias


def _dot(a, b):
    return jnp.dot(a, b, preferred_element_type=F32)


def _pack_rows(x):
    return pltpu.bitcast(x.astype(BF16), jnp.uint32)


def _unpack_rows(packed):
    return pltpu.bitcast(packed, BF16)


def _for_each_row_chunk(n_rows, fn):
    half = ROW_CHUNK // 2

    def body(c, carry):
        fn(pl.ds(pl.multiple_of(c * ROW_CHUNK, ROW_CHUNK), ROW_CHUNK),
           pl.ds(pl.multiple_of(c * half, half), half))
        return carry

    lax.fori_loop(0, n_rows // ROW_CHUNK, body, 0)


def _ffn_body(x_ref, gain_ref, wg_ref, wu_ref, wd_ref, post_gain_ref, *rest,
              final_norm, emit_next_norm, n_side):
    side_in = rest[:n_side]
    o_ref = rest[n_side]
    rest = rest[n_side + 1:]
    if emit_next_norm:
        next_ref, rest = rest[0], rest[1:]
    side_out, (xn_ref,) = rest[:n_side], rest[n_side:]
    k = pl.program_id(1)
    tm = x_ref.shape[0]

    @pl.when(k == 0)
    def _():
        def norm_chunk(rows, _):
            xn_ref[rows, :] = _rms(x_ref[rows, :], gain_ref[...]).astype(BF16)

        _for_each_row_chunk(tm, norm_chunk)
        o_ref[...] = jnp.zeros_like(o_ref)

    xn = xn_ref[...]
    gate = _dot(xn, wg_ref[...])
    up = _dot(xn, wu_ref[...])
    act = (jax.nn.silu(gate) * up).astype(BF16)
    for src_ref, dst_ref in zip(side_in, side_out):
        dst_ref[...] = src_ref[...].astype(BF16)
    o_ref[...] += _dot(act, wd_ref[...])

    @pl.when(k == pl.num_programs(1) - 1)
    def _():
        def finish_chunk(rows, packed_rows):
            y = x_ref[rows, :] + 0.5 * o_ref[rows, :]
            if final_norm:
                y = _rms(y, post_gain_ref[...])
            o_ref[rows, :] = y
            if emit_next_norm:
                next_ref[packed_rows, :] = _pack_rows(_rms(y, post_gain_ref[...]))

        _for_each_row_chunk(tm, finish_chunk)


def _side_block(shape, n_blocks):
    r, c = shape
    for n_col_blocks in range(1, n_blocks + 1):
        if n_blocks % n_col_blocks or c % n_col_blocks or r % (n_blocks // n_col_blocks):
            continue
        br, bc = r // (n_blocks // n_col_blocks), c // n_col_blocks
        if br % (2 * SUBLANES) == 0 and bc % LANES == 0:
            return br, bc
    raise ValueError(f"cannot cut {shape} into {n_blocks} aligned blocks")


def _ffn(x, gain, w_gu, w_down, post_gain, *, tm, tf, final_norm=False, emit_next_norm=False,
         side_cast=()):
    m, d = x.shape
    f = w_down.shape[0]
    nf = f // tf
    n_steps = (m // tm) * nf
    assert m % tm == 0 and f % tf == 0 and tm % ROW_CHUNK == 0
    assert not (final_norm and emit_next_norm)
    rows = lambda i, k: (i, 0)
    out_specs = [pl.BlockSpec((tm, d), rows)]
    out_shape = [jax.ShapeDtypeStruct((m, d), F32)]
    if emit_next_norm:
        out_specs.append(pl.BlockSpec((tm // 2, d), rows))
        out_shape.append(jax.ShapeDtypeStruct((m // 2, d), jnp.uint32))
    side_specs = []
    for w in side_cast:
        br, bc = _side_block(w.shape, n_steps)
        n_col_blocks = w.shape[1] // bc
        spec = pl.BlockSpec((br, bc), lambda i, k, n=n_col_blocks: ((i * nf + k) // n,
                                                                   (i * nf + k) % n))
        side_specs.append(spec)
        out_specs.append(spec)
        out_shape.append(jax.ShapeDtypeStruct(w.shape, BF16))
    return pl.pallas_call(
        functools.partial(_ffn_body, final_norm=final_norm, emit_next_norm=emit_next_norm,
                          n_side=len(side_cast)),
        grid=(m // tm, nf),
        in_specs=[
            pl.BlockSpec((tm, d), rows),
            pl.BlockSpec((1, d), lambda i, k: (0, 0)),
            pl.BlockSpec((d, tf), lambda i, k: (0, k)),
            pl.BlockSpec((d, tf), lambda i, k: (0, k + nf)),
            pl.BlockSpec((tf, d), lambda i, k: (k, 0)),
            pl.BlockSpec((1, d), lambda i, k: (0, 0)),
            *side_specs,
        ],
        out_specs=out_specs,
        out_shape=out_shape,
        scratch_shapes=[pltpu.VMEM((tm, d), BF16)],
        compiler_params=_params(),
        name="ffn",
    )(x, gain, w_gu, w_gu, w_down, post_gain, *side_cast)


def _section_outputs(pre, u_ref, gv_ref, ga_ref, gb_ref):
    u_ref[...] = jax.nn.gelu(pre(SEC_UA)).astype(BF16)
    gv_ref[...] = jax.nn.gelu(pre(SEC_VA))
    ga_ref[...] = jax.nn.sigmoid(pre(SEC_GATE_A)).astype(BF16)
    gb_ref[...] = jax.nn.sigmoid(pre(SEC_GATE_B)).astype(BF16)
    return pre(SEC_GLU_A) * jax.nn.sigmoid(pre(SEC_GLU_B))


def _causal_conv(gext_ref, cout_ref, wdw_ref, bdw_ref, c_ref):
    n_slab, tm, _ = cout_ref.shape
    k_conv = wdw_ref.shape[0]
    shift = CARRY_ROWS - (k_conv - 1)
    group = CONV_STRIDE * SUBLANES
    starts = [b * group + p for b in range(CONV_GROUPS) for p in range(CONV_STRIDE)]
    for s in range(n_slab):
        lanes = slice(s * LANES, (s + 1) * LANES)
        bias = jnp.broadcast_to(bdw_ref[:, lanes], (SUBLANES, LANES))
        for r0 in range(0, tm, CONV_GROUPS * group):
            accs = [bias for _ in starts]
            for k in range(k_conv):
                wk = jnp.broadcast_to(wdw_ref[k:k + 1, lanes], (SUBLANES, LANES))
                for q, st in enumerate(starts):
                    rows = pl.ds(r0 + st + shift + k, SUBLANES, stride=CONV_STRIDE)
                    accs[q] = accs[q] + wk * gext_ref[s, rows, :]
            for q, st in enumerate(starts):
                cout_ref[s, pl.ds(r0 + st, SUBLANES, stride=CONV_STRIDE), :] = accs[q]
        c_ref[:, lanes] = cout_ref[s]


def _proj_prompt_body(h_ref, w0, w1, w2, w3, w4, w5, wdw_ref, bdw_ref, *rest,
                      n_steps, tiles_per_seq, n_side):
    side_in, rest = rest[:n_side], rest[n_side:]
    u_ref, gv_ref, c_ref, ga_ref, gb_ref, tail_ref = rest[:6]
    side_out, rest = rest[6:6 + n_side], rest[6 + n_side:]
    raw_even_ref, raw_odd_ref, gext_ref, cout_ref, carry_ref = rest
    s = pl.program_id(0)
    n_slab, _, _ = cout_ref.shape
    tm = cout_ref.shape[1]
    w_refs = (w0, w1, w2, w3, w4, w5)
    prev_sub = jnp.maximum(s - 1, 0) % tiles_per_seq

    @pl.when(s == 0)
    def _():
        raw_odd_ref[...] = jnp.zeros_like(raw_odd_ref)

    @pl.when(prev_sub == 0)
    def _():
        carry_ref[...] = jnp.zeros_like(carry_ref)

    def step(raw_w, raw_r):
        for src_ref, dst_ref in zip(side_in, side_out):
            dst_ref[...] = src_ref[...].astype(BF16)
        h = _unpack_rows(h_ref[...])
        g = _section_outputs(lambda sec: raw_r[sec], u_ref, gv_ref, ga_ref, gb_ref)
        gext_ref[:, 0:CARRY_ROWS, :] = carry_ref[...]
        for sl in range(n_slab):
            gext_ref[sl, CARRY_ROWS:CARRY_ROWS + tm, :] = g[:, sl * LANES:(sl + 1) * LANES]
            carry_ref[sl] = g[tm - CARRY_ROWS:, sl * LANES:(sl + 1) * LANES]
        tail_ref[0] = g[tm - CARRY_ROWS:, :]
        _causal_conv(gext_ref, cout_ref, wdw_ref, bdw_ref, c_ref)

        for sec in range(N_SEC):
            raw_w[sec] = _dot(h, w_refs[sec][...])

    @pl.when(s % 2 == 0)
    def _():
        step(raw_even_ref, raw_odd_ref)

    @pl.when(s % 2 == 1)
    def _():
        step(raw_odd_ref, raw_even_ref)


def _proj_sample_body(h_ref, w0, w1, w2, w3, w4, w5, wdw_ref, bdw_ref, cache_ref,
                      u_ref, gv_ref, c_ref, ga_ref, gb_ref, g_ref, *, n_seq):
    w_refs = (w0, w1, w2, w3, w4, w5)
    k_conv = wdw_ref.shape[0]
    n_prev = k_conv - 1
    t_new = 2 * h_ref.shape[0] // n_seq

    h = _unpack_rows(h_ref[...])
    g_ref[...] = _section_outputs(lambda sec: _dot(h, w_refs[sec][...]),
                                  u_ref, gv_ref, ga_ref, gb_ref)

    bias = bdw_ref[...]
    for t in range(t_new):
        acc = jnp.broadcast_to(bias, (n_seq, bias.shape[1]))
        for k in range(k_conv):
            m = t + k
            if m < n_prev:
                slab = cache_ref[m]
            else:
                slab = g_ref[(m - n_prev) * n_seq:(m - n_prev + 1) * n_seq, :]
            acc = acc + wdw_ref[k:k + 1, :] * slab
        c_ref[t * n_seq:(t + 1) * n_seq, :] = acc


def _proj_out_shapes(m, dsec):
    return [
        jax.ShapeDtypeStruct((m, dsec), BF16),
        jax.ShapeDtypeStruct((m, dsec), F32),
        jax.ShapeDtypeStruct((m, dsec), F32),
        jax.ShapeDtypeStruct((m, dsec), BF16),
        jax.ShapeDtypeStruct((m, dsec), BF16),
    ]


def _proj_prompt(h, w_in, w_dw, b_dw, *, tm, tn, seq_len, side_cast=()):
    m, d = 2 * h.shape[0], h.shape[1]
    dsec = w_in.shape[1] // N_SEC
    nblk = dsec // tn
    n_slab = tn // LANES
    k_conv = w_dw.shape[0]
    assert m % tm == 0 and dsec % tn == 0 and tn % LANES == 0
    assert seq_len % tm == 0 and tm >= CARRY_ROWS
    assert tm % (CONV_GROUPS * CONV_STRIDE * SUBLANES) == 0
    tps = seq_len // tm
    n_steps = (m // tm) * nblk
    seq_of = lambda t: t // (nblk * tps)
    col_of = lambda t: (t // tps) % nblk
    row_tile_of = lambda t: seq_of(t) * tps + t % tps
    cur = lambda s: jnp.minimum(s, n_steps - 1)
    prev = lambda s: jnp.maximum(s - 1, 0)
    w_specs = [pl.BlockSpec((d, tn), lambda s, sec=sec: (0, sec * nblk + col_of(cur(s))))
               for sec in range(N_SEC)]
    out_tile = lambda s: (row_tile_of(prev(s)), col_of(prev(s)))
    side_specs, side_shapes = [], []
    for w in side_cast:
        br, bc = _side_block(w.shape, n_steps)
        n_col_blocks = w.shape[1] // bc
        side_specs.append(pl.BlockSpec(
            (br, bc), lambda s, n=n_col_blocks: (cur(s) // n, cur(s) % n)))
        side_shapes.append(jax.ShapeDtypeStruct(w.shape, BF16))
    return pl.pallas_call(
        functools.partial(_proj_prompt_body, n_steps=n_steps, tiles_per_seq=tps,
                          n_side=len(side_cast)),
        grid=(n_steps + 1,),
        in_specs=[
            pl.BlockSpec((tm // 2, d), lambda s: (row_tile_of(cur(s)), 0)),
            *w_specs,
            pl.BlockSpec((k_conv, tn), lambda s: (0, col_of(prev(s)))),
            pl.BlockSpec((1, tn), lambda s: (0, col_of(prev(s)))),
            *side_specs,
        ],
        out_specs=[pl.BlockSpec((tm, tn), out_tile) for _ in range(5)]
        + [pl.BlockSpec((1, CARRY_ROWS, tn),
                        lambda s: (row_tile_of(prev(s)), 0, col_of(prev(s))))]
        + side_specs,
        out_shape=_proj_out_shapes(m, dsec)
        + [jax.ShapeDtypeStruct((m // tm, CARRY_ROWS, dsec), F32)] + side_shapes,
        scratch_shapes=[
            pltpu.VMEM((N_SEC, tm, tn), F32),
            pltpu.VMEM((N_SEC, tm, tn), F32),
            pltpu.VMEM((n_slab, CARRY_ROWS + tm, LANES), F32),
            pltpu.VMEM((n_slab, tm, LANES), F32),
            pltpu.VMEM((n_slab, CARRY_ROWS, LANES), F32),
        ],
        compiler_params=_params(1),
        name="proj_prompt",
    )(h, *([w_in] * N_SEC), w_dw, b_dw, *side_cast)


def _proj_sample(h, w_in, w_dw, b_dw, cache_t, *, tn):
    m, d = 2 * h.shape[0], h.shape[1]
    dsec = w_in.shape[1] // N_SEC
    nblk = dsec // tn
    k_conv = w_dw.shape[0]
    n_seq = cache_t.shape[1]
    assert dsec % tn == 0 and m % n_seq == 0
    w_specs = [pl.BlockSpec((d, tn), lambda j, sec=sec: (0, sec * nblk + j))
               for sec in range(N_SEC)]
    return pl.pallas_call(
        functools.partial(_proj_sample_body, n_seq=n_seq),
        grid=(nblk,),
        in_specs=[
            pl.BlockSpec((m // 2, d), lambda j: (0, 0)),
            *w_specs,
            pl.BlockSpec((k_conv, tn), lambda j: (0, j)),
            pl.BlockSpec((1, tn), lambda j: (0, j)),
            pl.BlockSpec((k_conv - 1, n_seq, tn), lambda j: (0, 0, j)),
        ],
        out_specs=[pl.BlockSpec((m, tn), lambda j: (0, j)) for _ in range(6)],
        out_shape=_proj_out_shapes(m, dsec) + [jax.ShapeDtypeStruct((m, dsec), F32)],
        compiler_params=_params(1),
        name="proj_sample",
    )(h, *([w_in] * N_SEC), w_dw, b_dw, cache_t)


def _mixa_prompt_body(u_ref, gv_ref, c_ref, ws_ref, bs_ref, lnvg_ref, lnvb_ref, lncg_ref, lncb_ref,
                      a_ref, cc_ref, vst_ref, vb_ref):
    tr = u_ref.shape[0]
    n_head, chunk, _ = ws_ref.shape
    hd = u_ref.shape[1] // n_head
    for r in range(0, tr, ROW_CHUNK):
        rows = slice(r, r + ROW_CHUNK)
        v = _layer_norm(gv_ref[rows, :], lnvg_ref[...], lnvb_ref[...])
        if r >= tr - chunk:
            vst_ref[0, r - (tr - chunk):r - (tr - chunk) + ROW_CHUNK, :] = v
        vb_ref[rows, :] = v.astype(BF16)
        cc_ref[r // 2:(r + ROW_CHUNK) // 2, :] = _pack_rows(jax.nn.silu(
            _layer_norm(c_ref[rows, :], lncg_ref[...], lncb_ref[...])))
    row = lax.broadcasted_iota(jnp.int32, (chunk, chunk), 0)
    col = lax.broadcasted_iota(jnp.int32, (chunk, chunk), 1)
    causal = (row >= col).astype(F32)
    for h in range(n_head):
        w = (ws_ref[h] * causal).astype(BF16)
        cols = slice(h * hd, (h + 1) * hd)
        for c in range(tr // chunk):
            rows = slice(c * chunk, (c + 1) * chunk)
            s = _dot(w, vb_ref[rows, cols]) + bs_ref[:, cols]
            packed_rows = slice(c * chunk // 2, (c + 1) * chunk // 2)
            a_ref[packed_rows, cols] = _pack_rows(u_ref[rows, cols].astype(F32) * s)


def _mixa_sample_body(u_ref, gv_ref, c_ref, coef_ref, bs_ref, lnvg_ref, lnvb_ref, lncg_ref,
                      lncb_ref, a_ref, cc_ref, v_ref, *, n_seq):
    t_new = u_ref.shape[0] // n_seq
    v_ref[...] = _layer_norm(gv_ref[...], lnvg_ref[...], lnvb_ref[...])
    for i in range(t_new):
        rows_i = slice(i * n_seq, (i + 1) * n_seq)
        s = jnp.broadcast_to(bs_ref[i:i + 1, :], (n_seq, bs_ref.shape[1]))
        for j in range(i + 1):
            s = s + coef_ref[i, j:j + 1, :] * v_ref[j * n_seq:(j + 1) * n_seq, :]
        a_ref[i * n_seq // 2:(i + 1) * n_seq // 2, :] = _pack_rows(
            u_ref[rows_i, :].astype(F32) * s)
    cc_ref[...] = _pack_rows(
        jax.nn.silu(_layer_norm(c_ref[...], lncg_ref[...], lncb_ref[...])))


def _mixa_prompt(u, gv, c, w_s, bs_slab, ln_v_g, ln_v_b, ln_c_g, ln_c_b, *, tr, seq_len):
    m, d = u.shape
    n_head, chunk, _ = w_s.shape
    assert seq_len % tr == 0 and tr % chunk == 0 and chunk % ROW_CHUNK == 0
    tiles_per_seq = seq_len // tr
    row = lambda i: (i, 0)
    const2 = lambda i: (0, 0)
    return pl.pallas_call(
        _mixa_prompt_body,
        grid=(m // tr,),
        in_specs=[
            pl.BlockSpec((tr, d), row),
            pl.BlockSpec((tr, d), row),
            pl.BlockSpec((tr, d), row),
            pl.BlockSpec((n_head, chunk, chunk), lambda i: (0, 0, 0)),
            pl.BlockSpec((chunk, d), const2),
            pl.BlockSpec((1, d), const2),
            pl.BlockSpec((1, d), const2),
            pl.BlockSpec((1, d), const2),
            pl.BlockSpec((1, d), const2),
        ],
        out_specs=[
            pl.BlockSpec((tr // 2, d), row),
            pl.BlockSpec((tr // 2, d), row),
            pl.BlockSpec((1, chunk, d), lambda i: (i // tiles_per_seq, 0, 0)),
        ],
        out_shape=[
            jax.ShapeDtypeStruct((m // 2, d), jnp.uint32),
            jax.ShapeDtypeStruct((m // 2, d), jnp.uint32),
            jax.ShapeDtypeStruct((m // seq_len, chunk, d), F32),
        ],
        scratch_shapes=[pltpu.VMEM((tr, d), BF16)],
        compiler_params=_params(1),
        name="mixa_prompt",
    )(u, gv, c, w_s, bs_slab, ln_v_g, ln_v_b, ln_c_g, ln_c_b)


def _mixa_sample(u, gv, c, coef, bs_rows, ln_v_g, ln_v_b, ln_c_g, ln_c_b, *, n_seq):
    m, d = u.shape
    t_new = m // n_seq
    whole = lambda i: (0, 0)
    return pl.pallas_call(
        functools.partial(_mixa_sample_body, n_seq=n_seq),
        grid=(1,),
        in_specs=[
            pl.BlockSpec((m, d), whole),
            pl.BlockSpec((m, d), whole),
            pl.BlockSpec((m, d), whole),
            pl.BlockSpec((t_new, t_new, d), lambda i: (0, 0, 0)),
            pl.BlockSpec((t_new, d), whole),
            pl.BlockSpec((1, d), whole),
            pl.BlockSpec((1, d), whole),
            pl.BlockSpec((1, d), whole),
            pl.BlockSpec((1, d), whole),
        ],
        out_specs=[pl.BlockSpec((m // 2, d), whole), pl.BlockSpec((m // 2, d), whole),
                   pl.BlockSpec((m, d), whole)],
        out_shape=[
            jax.ShapeDtypeStruct((m // 2, d), jnp.uint32),
            jax.ShapeDtypeStruct((m // 2, d), jnp.uint32),
            jax.ShapeDtypeStruct((m, d), F32),
        ],
        compiler_params=_params(1),
        name="mixa_sample",
    )(u, gv, c, coef, bs_rows, ln_v_g, ln_v_b, ln_c_g, ln_c_b)


def _mixb_body(x_ref, a_ref, cc_ref, ga_ref, gb_ref, wpa_ref, wpb_ref, wout_ref, o_ref):
    j = pl.program_id(1)

    @pl.when(j == 0)
    def _():
        o_ref[...] = jnp.zeros_like(o_ref)

    ya = _dot(_unpack_rows(a_ref[...]), wpa_ref[...])
    yb = _dot(_unpack_rows(cc_ref[...]), wpb_ref[...])
    merged = (ga_ref[...].astype(F32) * ya + gb_ref[...].astype(F32) * yb).astype(BF16)
    o_ref[...] += _dot(merged, wout_ref[...])

    @pl.when(j == pl.num_programs(1) - 1)
    def _():
        def add_chunk(rows, _):
            o_ref[rows, :] = x_ref[rows, :] + o_ref[rows, :]

        _for_each_row_chunk(x_ref.shape[0], add_chunk)


def _mixb(x, a, cc, ga, gb, w_pa, w_pb, w_out, *, tr, tj):
    m, d = x.shape
    dm = w_pa.shape[1]
    assert m % tr == 0 and dm % tj == 0 and tr % ROW_CHUNK == 0
    row = lambda i, j: (i, 0)
    return pl.pallas_call(
        _mixb_body,
        grid=(m // tr, dm // tj),
        in_specs=[
            pl.BlockSpec((tr, d), row),
            pl.BlockSpec((tr // 2, a.shape[1]), row),
            pl.BlockSpec((tr // 2, cc.shape[1]), row),
            pl.BlockSpec((tr, tj), lambda i, j: (i, j)),
            pl.BlockSpec((tr, tj), lambda i, j: (i, j)),
            pl.BlockSpec((w_pa.shape[0], tj), lambda i, j: (0, j)),
            pl.BlockSpec((w_pb.shape[0], tj), lambda i, j: (0, j)),
            pl.BlockSpec((tj, d), lambda i, j: (j, 0)),
        ],
        out_specs=pl.BlockSpec((tr, d), row),
        out_shape=jax.ShapeDtypeStruct((m, d), F32),
        compiler_params=_params(),
        name="mixb",
    )(x, a, cc, ga, gb, w_pa, w_pb, w_out)


def kernel(x_prompt, x_sample, cache_conv, ffn1_norm, ffn1_w_gu, ffn1_w_down, mix_norm, w_in, w_s, b_s, ln_v_g, ln_v_b, w_pa, w_dw, b_dw, ln_c_g, ln_c_b, w_pb, w_out, ffn2_norm, ffn2_w_gu, ffn2_w_down, final_norm):
    n_p, t_p, d = x_prompt.shape
    n_s, t_s, _ = x_sample.shape
    depth, n_head, chunk, _ = w_s.shape
    d_a = ln_v_g.shape[1]
    hd = d_a // n_head
    k_conv = w_dw.shape[1]
    assert t_s <= chunk and k_conv - 1 <= CARRY_ROWS and n_s % (2 * SUBLANES) == 0
    last_len = t_p - chunk * ((t_p - 1) // chunk)
    assert last_len == chunk, "prompt length must be a whole number of chunks"

    xp = x_prompt.reshape(n_p * t_p, d)
    xs = jnp.swapaxes(x_sample, 0, 1).reshape(t_s * n_s, d)
    m_s = t_s * n_s
    row2 = lambda v: v.reshape(1, -1)
    fgain = row2(final_norm)

    tiles = _tile_sizes()
    tiles_per_seq = t_p // tiles.proj_rows
    vp_list, vs_list, cp_list, cs_list = [], [], [], []
    for l in range(depth):
        last = l == depth - 1
        w1gu, w1d = ffn1_w_gu[l].astype(BF16), ffn1_w_down[l].astype(BF16)
        win = w_in[l].astype(BF16)
        g1, gm, g2 = row2(ffn1_norm[l]), row2(mix_norm[l]), row2(ffn2_norm[l])
        bdw = row2(b_dw[l])
        lnvg, lnvb, lncg, lncb = row2(ln_v_g[l]), row2(ln_v_b[l]), row2(ln_c_g[l]), row2(ln_c_b[l])
        bs_slab = jnp.repeat(b_s[l].T, hd, axis=1)
        coef = jnp.repeat(jnp.transpose(w_s[l][:, :t_s, :t_s], (1, 2, 0)), hd, axis=2)
        cache_t = jnp.swapaxes(cache_conv[l], 0, 1)

        xp, hp, w2gu, w2d = _ffn(xp, g1, w1gu, w1d, gm, tm=tiles.ffn_rows,
                                 tf=tiles.ffn_hidden_emit, emit_next_norm=True,
                                 side_cast=(ffn2_w_gu[l], ffn2_w_down[l]))
        u, gv, c, ga, gb, tail, wpa, wpb, wout = _proj_prompt(
            hp, win, w_dw[l], bdw, tm=tiles.proj_rows, tn=tiles.proj_cols, seq_len=t_p,
            side_cast=(w_pa[l], w_pb[l], w_out[l]))
        tail = tail[tiles_per_seq - 1::tiles_per_seq]
        a, cc, v_p = _mixa_prompt(u, gv, c, w_s[l], bs_slab, lnvg, lnvb, lncg, lncb,
                                  tr=tiles.mixa_rows, seq_len=t_p)
        xp = _mixb(xp, a, cc, ga, gb, wpa, wpb, wout, tr=tiles.mix_rows, tj=tiles.mix_cols)
        (xp,) = _ffn(xp, g2, w2gu, w2d, fgain, tm=tiles.ffn_rows, tf=tiles.ffn_hidden,
                     final_norm=last)
        vp_list.append(v_p)
        cp_list.append(tail[:, CARRY_ROWS - (k_conv - 1):, :])

        xs, hs = _ffn(xs, g1, w1gu, w1d, gm, tm=m_s, tf=tiles.ffn_hidden, emit_next_norm=True)
        u, gv, c, ga, gb, g_s = _proj_sample(hs, win, w_dw[l], bdw, cache_t, tn=tiles.proj_cols)
        a, cc, v_s = _mixa_sample(u, gv, c, coef, bs_slab[:t_s], lnvg, lnvb, lncg, lncb, n_seq=n_s)
        xs = _mixb(xs, a, cc, ga, gb, wpa, wpb, wout, tr=m_s, tj=tiles.mix_cols)
        (xs,) = _ffn(xs, g2, w2gu, w2d, fgain, tm=m_s, tf=tiles.ffn_hidden, final_norm=last)
        vs_list.append(jnp.swapaxes(v_s.reshape(t_s, n_s, d_a), 0, 1))
        g_seq = jnp.swapaxes(g_s.reshape(t_s, n_s, -1), 0, 1)
        cs_list.append(jnp.concatenate([cache_conv[l], g_seq], axis=1)[:, -(k_conv - 1):])

    y_prompt = xp.reshape(n_p, t_p, d)
    y_sample = jnp.swapaxes(xs.reshape(t_s, n_s, d), 0, 1)
    return (y_prompt, y_sample, jnp.stack(vp_list), jnp.stack(vs_list),
            jnp.stack(cp_list), jnp.stack(cs_list))
```

```python
import functools
from typing import NamedTuple

import jax
import jax.numpy as jnp
from jax import lax
from jax.experimental import pallas as pl
from jax.experimental.pallas import tpu as pltpu

F32 = jnp.float32
BF16 = jnp.bfloat16
EPS = 1e-6

V7X_VMEM_BYTES = 64 * 1024 * 1024
VMEM_LIMIT_BYTES = V7X_VMEM_BYTES - 4 * 1024 * 1024
SUBLANES = 8
LANES = 128
CONV_STRIDE = 2
CONV_GROUPS = 4
CARRY_ROWS = 32
ROW_CHUNK = 64

SEC_UA, SEC_VA, SEC_GLU_A, SEC_GLU_B, SEC_GATE_A, SEC_GATE_B = range(6)
N_SEC = 6


class _Tiles(NamedTuple):
    ffn_rows: int
    ffn_hidden: int
    ffn_hidden_emit: int
    proj_rows: int
    proj_cols: int
    mixa_rows: int
    mix_rows: int
    mix_cols: int


def _tile_sizes():
    return _Tiles(ffn_rows=1024, ffn_hidden=512, ffn_hidden_emit=256, proj_rows=512, proj_cols=256,
                  mixa_rows=256, mix_rows=1024, mix_cols=256)


def _params(n_axes=2):
    return pltpu.CompilerParams(
        dimension_semantics=("arbitrary",) * n_axes,
        vmem_limit_bytes=VMEM_LIMIT_BYTES,
    )


def _rms(x, gain):
    ms = jnp.mean(x * x, axis=-1, keepdims=True)
    return x * lax.rsqrt(ms + EPS) * gain


def _layer_norm(x, gain, bias):
    mu = jnp.mean(x, axis=-1, keepdims=True)
    xc = x - mu
    var = jnp.mean(xc * xc, axis=-1, keepdims=True)
    return xc * lax.rsqrt(var + EPS) * gain + bias


def _dot(a, b):
    return jnp.dot(a, b, preferred_element_type=F32)


def _pack_rows(x):
    return pltpu.bitcast(x.astype(BF16), jnp.uint32)


def _unpack_rows(packed):
    return pltpu.bitcast(packed, BF16)


def _for_each_row_chunk(n_rows, fn):
    half = ROW_CHUNK // 2

    def body(c, carry):
        fn(pl.ds(pl.multiple_of(c * ROW_CHUNK, ROW_CHUNK), ROW_CHUNK),
           pl.ds(pl.multiple_of(c * half, half), half))
        return carry

    lax.fori_loop(0, n_rows // ROW_CHUNK, body, 0, unroll=2)


def _ffn_body(x_ref, gain_ref, wg_ref, wu_ref, wd_ref, post_gain_ref, *rest,
              final_norm, emit_next_norm, n_side):
    side_in = rest[:n_side]
    o_ref = rest[n_side]
    rest = rest[n_side + 1:]
    if emit_next_norm:
        next_ref, rest = rest[0], rest[1:]
    side_out, (xn_ref,) = rest[:n_side], rest[n_side:]
    k = pl.program_id(1)
    tm = x_ref.shape[0]

    @pl.when(k == 0)
    def _():
        def norm_chunk(rows, _):
            xn_ref[rows, :] = _rms(x_ref[rows, :], gain_ref[...]).astype(BF16)
            o_ref[rows, :] = jnp.zeros((ROW_CHUNK, o_ref.shape[1]), F32)

        _for_each_row_chunk(tm, norm_chunk)

    xn = xn_ref[...]
    gate = _dot(xn, wg_ref[...])
    up = _dot(xn, wu_ref[...])
    act = (jax.nn.silu(gate) * up).astype(BF16)
    for src_ref, dst_ref in zip(side_in, side_out):
        dst_ref[...] = src_ref[...].astype(BF16)
    o_ref[...] += _dot(act, wd_ref[...])

    @pl.when(k == pl.num_programs(1) - 1)
    def _():
        def finish_chunk(rows, packed_rows):
            y = x_ref[rows, :] + 0.5 * o_ref[rows, :]
            if final_norm:
                y = _rms(y, post_gain_ref[...])
            o_ref[rows, :] = y
            if emit_next_norm:
                next_ref[packed_rows, :] = _pack_rows(_rms(y, post_gain_ref[...]))

        _for_each_row_chunk(tm, finish_chunk)


def _side_block(shape, n_blocks):
    r, c = shape
    for n_col_blocks in range(1, n_blocks + 1):
        if n_blocks % n_col_blocks or c % n_col_blocks or r % (n_blocks // n_col_blocks):
            continue
        br, bc = r // (n_blocks // n_col_blocks), c // n_col_blocks
        if br % (2 * SUBLANES) == 0 and bc % LANES == 0:
            return br, bc
    raise ValueError(f"cannot cut {shape} into {n_blocks} aligned blocks")


def _ffn(x, gain, w_gu, w_down, post_gain, *, tm, tf, final_norm=False, emit_next_norm=False,
         side_cast=()):
    m, d = x.shape
    f = w_down.shape[0]
    nf = f // tf
    n_steps = (m // tm) * nf
    assert m % tm == 0 and f % tf == 0 and tm % ROW_CHUNK == 0
    assert not (final_norm and emit_next_norm)
    rows = lambda i, k: (i, 0)
    out_specs = [pl.BlockSpec((tm, d), rows)]
    out_shape = [jax.ShapeDtypeStruct((m, d), F32)]
    if emit_next_norm:
        out_specs.append(pl.BlockSpec((tm // 2, d), rows))
        out_shape.append(jax.ShapeDtypeStruct((m // 2, d), jnp.uint32))
    side_specs = []
    for w in side_cast:
        br, bc = _side_block(w.shape, n_steps)
        n_col_blocks = w.shape[1] // bc
        spec = pl.BlockSpec((br, bc), lambda i, k, n=n_col_blocks: ((i * nf + k) // n,
                                                                   (i * nf + k) % n))
        side_specs.append(spec)
        out_specs.append(spec)
        out_shape.append(jax.ShapeDtypeStruct(w.shape, BF16))
    return pl.pallas_call(
        functools.partial(_ffn_body, final_norm=final_norm, emit_next_norm=emit_next_norm,
                          n_side=len(side_cast)),
        grid=(m // tm, nf),
        in_specs=[
            pl.BlockSpec((tm, d), rows),
            pl.BlockSpec((1, d), lambda i, k: (0, 0)),
            pl.BlockSpec((d, tf), lambda i, k: (0, k)),
            pl.BlockSpec((d, tf), lambda i, k: (0, k + nf)),
            pl.BlockSpec((tf, d), lambda i, k: (k, 0)),
            pl.BlockSpec((1, d), lambda i, k: (0, 0)),
            *side_specs,
        ],
        out_specs=out_specs,
        out_shape=out_shape,
        scratch_shapes=[pltpu.VMEM((tm, d), BF16)],
        compiler_params=_params(),
        name="ffn",
    )(x, gain, w_gu, w_gu, w_down, post_gain, *side_cast)


def _section_outputs(pre, u_ref, gv_ref, ga_ref, gb_ref):
    u_ref[...] = jax.nn.gelu(pre(SEC_UA)).astype(BF16)
    gv_ref[...] = jax.nn.gelu(pre(SEC_VA))
    ga_ref[...] = jax.nn.sigmoid(pre(SEC_GATE_A)).astype(BF16)
    gb_ref[...] = jax.nn.sigmoid(pre(SEC_GATE_B)).astype(BF16)
    return pre(SEC_GLU_A) * jax.nn.sigmoid(pre(SEC_GLU_B))


def _causal_conv(gext_ref, cout_ref, wdw_ref, bdw_ref, c_ref):
    n_slab, tm, _ = cout_ref.shape
    k_conv = wdw_ref.shape[0]
    shift = CARRY_ROWS - (k_conv - 1)
    group = CONV_STRIDE * SUBLANES
    starts = [b * group + p for b in range(CONV_GROUPS) for p in range(CONV_STRIDE)]
    for s in range(n_slab):
        lanes = slice(s * LANES, (s + 1) * LANES)
        bias = jnp.broadcast_to(bdw_ref[:, lanes], (SUBLANES, LANES))
        for r0 in range(0, tm, CONV_GROUPS * group):
            accs = [bias for _ in starts]
            for k in range(k_conv):
                wk = jnp.broadcast_to(wdw_ref[k:k + 1, lanes], (SUBLANES, LANES))
                for q, st in enumerate(starts):
                    rows = pl.ds(r0 + st + shift + k, SUBLANES, stride=CONV_STRIDE)
                    accs[q] = accs[q] + wk * gext_ref[s, rows, :]
            for q, st in enumerate(starts):
                cout_ref[s, pl.ds(r0 + st, SUBLANES, stride=CONV_STRIDE), :] = accs[q]
        c_ref[:, lanes] = cout_ref[s]


def _proj_prompt_body(h_ref, w0, w1, w2, w3, w4, w5, wdw_ref, bdw_ref, *rest,
                      n_steps, tiles_per_seq, n_side):
    side_in, rest = rest[:n_side], rest[n_side:]
    u_ref, gv_ref, c_ref, ga_ref, gb_ref, tail_ref = rest[:6]
    side_out, rest = rest[6:6 + n_side], rest[6 + n_side:]
    raw_even_ref, raw_odd_ref, gext_ref, cout_ref, carry_ref = rest
    s = pl.program_id(0)
    n_slab, _, _ = cout_ref.shape
    tm = cout_ref.shape[1]
    w_refs = (w0, w1, w2, w3, w4, w5)
    prev_sub = jnp.maximum(s - 1, 0) % tiles_per_seq

    @pl.when(s == 0)
    def _():
        raw_odd_ref[...] = jnp.zeros_like(raw_odd_ref)

    @pl.when(prev_sub == 0)
    def _():
        carry_ref[...] = jnp.zeros_like(carry_ref)

    def step(raw_w, raw_r):
        for src_ref, dst_ref in zip(side_in, side_out):
            dst_ref[...] = src_ref[...].astype(BF16)
        h = _unpack_rows(h_ref[...])
        g = _section_outputs(lambda sec: raw_r[sec], u_ref, gv_ref, ga_ref, gb_ref)
        gext_ref[:, 0:CARRY_ROWS, :] = carry_ref[...]
        for sl in range(n_slab):
            gext_ref[sl, CARRY_ROWS:CARRY_ROWS + tm, :] = g[:, sl * LANES:(sl + 1) * LANES]
            carry_ref[sl] = g[tm - CARRY_ROWS:, sl * LANES:(sl + 1) * LANES]
        tail_ref[0] = g[tm - CARRY_ROWS:, :]
        _causal_conv(gext_ref, cout_ref, wdw_ref, bdw_ref, c_ref)

        for sec in range(N_SEC):
            raw_w[sec] = _dot(h, w_refs[sec][...])

    @pl.when(s % 2 == 0)
    def _():
        step(raw_even_ref, raw_odd_ref)

    @pl.when(s % 2 == 1)
    def _():
        step(raw_odd_ref, raw_even_ref)


def _proj_sample_body(h_ref, w0, w1, w2, w3, w4, w5, wdw_ref, bdw_ref, cache_ref,
                      u_ref, gv_ref, c_ref, ga_ref, gb_ref, g_ref, *, n_seq):
    w_refs = (w0, w1, w2, w3, w4, w5)
    k_conv = wdw_ref.shape[0]
    n_prev = k_conv - 1
    t_new = 2 * h_ref.shape[0] // n_seq

    h = _unpack_rows(h_ref[...])
    g_ref[...] = _section_outputs(lambda sec: _dot(h, w_refs[sec][...]),
                                  u_ref, gv_ref, ga_ref, gb_ref)

    bias = bdw_ref[...]
    for t in range(t_new):
        acc = jnp.broadcast_to(bias, (n_seq, bias.shape[1]))
        for k in range(k_conv):
            m = t + k
            if m < n_prev:
                slab = cache_ref[m]
            else:
                slab = g_ref[(m - n_prev) * n_seq:(m - n_prev + 1) * n_seq, :]
            acc = acc + wdw_ref[k:k + 1, :] * slab
        c_ref[t * n_seq:(t + 1) * n_seq, :] = acc


def _proj_out_shapes(m, dsec):
    return [
        jax.ShapeDtypeStruct((m, dsec), BF16),
        jax.ShapeDtypeStruct((m, dsec), F32),
        jax.ShapeDtypeStruct((m, dsec), F32),
        jax.ShapeDtypeStruct((m, dsec), BF16),
        jax.ShapeDtypeStruct((m, dsec), BF16),
    ]


def _proj_prompt(h, w_in, w_dw, b_dw, *, tm, tn, seq_len, side_cast=()):
    m, d = 2 * h.shape[0], h.shape[1]
    dsec = w_in.shape[1] // N_SEC
    nblk = dsec // tn
    n_slab = tn // LANES
    k_conv = w_dw.shape[0]
    assert m % tm == 0 and dsec % tn == 0 and tn % LANES == 0
    assert seq_len % tm == 0 and tm >= CARRY_ROWS
    assert tm % (CONV_GROUPS * CONV_STRIDE * SUBLANES) == 0
    tps = seq_len // tm
    n_steps = (m // tm) * nblk
    seq_of = lambda t: t // (nblk * tps)
    col_of = lambda t: (t // tps) % nblk
    row_tile_of = lambda t: seq_of(t) * tps + t % tps
    cur = lambda s: jnp.minimum(s, n_steps - 1)
    prev = lambda s: jnp.maximum(s - 1, 0)
    w_specs = [pl.BlockSpec((d, tn), lambda s, sec=sec: (0, sec * nblk + col_of(cur(s))))
               for sec in range(N_SEC)]
    out_tile = lambda s: (row_tile_of(prev(s)), col_of(prev(s)))
    side_specs, side_shapes = [], []
    for w in side_cast:
        br, bc = _side_block(w.shape, n_steps)
        n_col_blocks = w.shape[1] // bc
        side_specs.append(pl.BlockSpec(
            (br, bc), lambda s, n=n_col_blocks: (cur(s) // n, cur(s) % n)))
        side_shapes.append(jax.ShapeDtypeStruct(w.shape, BF16))
    return pl.pallas_call(
        functools.partial(_proj_prompt_body, n_steps=n_steps, tiles_per_seq=tps,
                          n_side=len(side_cast)),
        grid=(n_steps + 1,),
        in_specs=[
            pl.BlockSpec((tm // 2, d), lambda s: (row_tile_of(cur(s)), 0)),
            *w_specs,
            pl.BlockSpec((k_conv, tn), lambda s: (0, col_of(prev(s)))),
            pl.BlockSpec((1, tn), lambda s: (0, col_of(prev(s)))),
            *side_specs,
        ],
        out_specs=[pl.BlockSpec((tm, tn), out_tile) for _ in range(5)]
        + [pl.BlockSpec((1, CARRY_ROWS, tn),
                        lambda s: (row_tile_of(prev(s)), 0, col_of(prev(s))))]
        + side_specs,
        out_shape=_proj_out_shapes(m, dsec)
        + [jax.ShapeDtypeStruct((m // tm, CARRY_ROWS, dsec), F32)] + side_shapes,
        scratch_shapes=[
            pltpu.VMEM((N_SEC, tm, tn), F32),
            pltpu.VMEM((N_SEC, tm, tn), F32),
            pltpu.VMEM((n_slab, CARRY_ROWS + tm, LANES), F32),
            pltpu.VMEM((n_slab, tm, LANES), F32),
            pltpu.VMEM((n_slab, CARRY_ROWS, LANES), F32),
        ],
        compiler_params=_params(1),
        name="proj_prompt",
    )(h, *([w_in] * N_SEC), w_dw, b_dw, *side_cast)


def _proj_sample(h, w_in, w_dw, b_dw, cache_t, *, tn):
    m, d = 2 * h.shape[0], h.shape[1]
    dsec = w_in.shape[1] // N_SEC
    nblk = dsec // tn
    k_conv = w_dw.shape[0]
    n_seq = cache_t.shape[1]
    assert dsec % tn == 0 and m % n_seq == 0
    w_specs = [pl.BlockSpec((d, tn), lambda j, sec=sec: (0, sec * nblk + j))
               for sec in range(N_SEC)]
    return pl.pallas_call(
        functools.partial(_proj_sample_body, n_seq=n_seq),
        grid=(nblk,),
        in_specs=[
            pl.BlockSpec((m // 2, d), lambda j: (0, 0)),
            *w_specs,
            pl.BlockSpec((k_conv, tn), lambda j: (0, j)),
            pl.BlockSpec((1, tn), lambda j: (0, j)),
            pl.BlockSpec((k_conv - 1, n_seq, tn), lambda j: (0, 0, j)),
        ],
        out_specs=[pl.BlockSpec((m, tn), lambda j: (0, j)) for _ in range(6)],
        out_shape=_proj_out_shapes(m, dsec) + [jax.ShapeDtypeStruct((m, dsec), F32)],
        compiler_params=_params(1),
        name="proj_sample",
    )(h, *([w_in] * N_SEC), w_dw, b_dw, cache_t)


def _mixa_prompt_body(u_ref, gv_ref, c_ref, ws_ref, bs_ref, lnvg_ref, lnvb_ref, lncg_ref, lncb_ref,
                      a_ref, cc_ref, vst_ref, vb_ref):
    tr = u_ref.shape[0]
    n_head, chunk, _ = ws_ref.shape
    hd = u_ref.shape[1] // n_head
    for r in range(0, tr, ROW_CHUNK):
        rows = slice(r, r + ROW_CHUNK)
        v = _layer_norm(gv_ref[rows, :], lnvg_ref[...], lnvb_ref[...])
        if r >= tr - chunk:
            vst_ref[0, r - (tr - chunk):r - (tr - chunk) + ROW_CHUNK, :] = v
        vb_ref[rows, :] = v.astype(BF16)
        cc_ref[r // 2:(r + ROW_CHUNK) // 2, :] = _pack_rows(jax.nn.silu(
            _layer_norm(c_ref[rows, :], lncg_ref[...], lncb_ref[...])))
    row = lax.broadcasted_iota(jnp.int32, (chunk, chunk), 0)
    col = lax.broadcasted_iota(jnp.int32, (chunk, chunk), 1)
    causal = (row >= col).astype(F32)
    for h in range(n_head):
        w = (ws_ref[h] * causal).astype(BF16)
        cols = slice(h * hd, (h + 1) * hd)
        for c in range(tr // chunk):
            rows = slice(c * chunk, (c + 1) * chunk)
            s = _dot(w, vb_ref[rows, cols]) + bs_ref[:, cols]
            packed_rows = slice(c * chunk // 2, (c + 1) * chunk // 2)
            a_ref[packed_rows, cols] = _pack_rows(u_ref[rows, cols].astype(F32) * s)


def _mixa_sample_body(u_ref, gv_ref, c_ref, coef_ref, bs_ref, lnvg_ref, lnvb_ref, lncg_ref,
                      lncb_ref, a_ref, cc_ref, v_ref, *, n_seq):
    t_new = u_ref.shape[0] // n_seq
    v_ref[...] = _layer_norm(gv_ref[...], lnvg_ref[...], lnvb_ref[...])
    for i in range(t_new):
        rows_i = slice(i * n_seq, (i + 1) * n_seq)
        s = jnp.broadcast_to(bs_ref[i:i + 1, :], (n_seq, bs_ref.shape[1]))
        for j in range(i + 1):
            s = s + coef_ref[i, j:j + 1, :] * v_ref[j * n_seq:(j + 1) * n_seq, :]
        a_ref[i * n_seq // 2:(i + 1) * n_seq // 2, :] = _pack_rows(
            u_ref[rows_i, :].astype(F32) * s)
    cc_ref[...] = _pack_rows(
        jax.nn.silu(_layer_norm(c_ref[...], lncg_ref[...], lncb_ref[...])))


def _mixa_prompt(u, gv, c, w_s, bs_slab, ln_v_g, ln_v_b, ln_c_g, ln_c_b, *, tr, seq_len):
    m, d = u.shape
    n_head, chunk, _ = w_s.shape
    assert seq_len % tr == 0 and tr % chunk == 0 and chunk % ROW_CHUNK == 0
    tiles_per_seq = seq_len // tr
    row = lambda i: (i, 0)
    const2 = lambda i: (0, 0)
    return pl.pallas_call(
        _mixa_prompt_body,
        grid=(m // tr,),
        in_specs=[
            pl.BlockSpec((tr, d), row),
            pl.BlockSpec((tr, d), row),
            pl.BlockSpec((tr, d), row),
            pl.BlockSpec((n_head, chunk, chunk), lambda i: (0, 0, 0)),
            pl.BlockSpec((chunk, d), const2),
            pl.BlockSpec((1, d), const2),
            pl.BlockSpec((1, d), const2),
            pl.BlockSpec((1, d), const2),
            pl.BlockSpec((1, d), const2),
        ],
        out_specs=[
            pl.BlockSpec((tr // 2, d), row),
            pl.BlockSpec((tr // 2, d), row),
            pl.BlockSpec((1, chunk, d), lambda i: (i // tiles_per_seq, 0, 0)),
        ],
        out_shape=[
            jax.ShapeDtypeStruct((m // 2, d), jnp.uint32),
            jax.ShapeDtypeStruct((m // 2, d), jnp.uint32),
            jax.ShapeDtypeStruct((m // seq_len, chunk, d), F32),
        ],
        scratch_shapes=[pltpu.VMEM((tr, d), BF16)],
        compiler_params=_params(1),
        name="mixa_prompt",
    )(u, gv, c, w_s, bs_slab, ln_v_g, ln_v_b, ln_c_g, ln_c_b)


def _mixa_sample(u, gv, c, coef, bs_rows, ln_v_g, ln_v_b, ln_c_g, ln_c_b, *, n_seq):
    m, d = u.shape
    t_new = m // n_seq
    whole = lambda i: (0, 0)
    return pl.pallas_call(
        functools.partial(_mixa_sample_body, n_seq=n_seq),
        grid=(1,),
        in_specs=[
            pl.BlockSpec((m, d), whole),
            pl.BlockSpec((m, d), whole),
            pl.BlockSpec((m, d), whole),
            pl.BlockSpec((t_new, t_new, d), lambda i: (0, 0, 0)),
            pl.BlockSpec((t_new, d), whole),
            pl.BlockSpec((1, d), whole),
            pl.BlockSpec((1, d), whole),
            pl.BlockSpec((1, d), whole),
            pl.BlockSpec((1, d), whole),
        ],
        out_specs=[pl.BlockSpec((m // 2, d), whole), pl.BlockSpec((m // 2, d), whole),
                   pl.BlockSpec((m, d), whole)],
        out_shape=[
            jax.ShapeDtypeStruct((m // 2, d), jnp.uint32),
            jax.ShapeDtypeStruct((m // 2, d), jnp.uint32),
            jax.ShapeDtypeStruct((m, d), F32),
        ],
        compiler_params=_params(1),
        name="mixa_sample",
    )(u, gv, c, coef, bs_rows, ln_v_g, ln_v_b, ln_c_g, ln_c_b)


def _mixb_body(x_ref, a_ref, cc_ref, ga_ref, gb_ref, wpa_ref, wpb_ref, wout_ref, o_ref):
    j = pl.program_id(1)

    @pl.when(j == 0)
    def _():
        o_ref[...] = x_ref[...]

    ya = _dot(_unpack_rows(a_ref[...]), wpa_ref[...])
    yb = _dot(_unpack_rows(cc_ref[...]), wpb_ref[...])
    merged = (ga_ref[...].astype(F32) * ya + gb_ref[...].astype(F32) * yb).astype(BF16)
    o_ref[...] += _dot(merged, wout_ref[...])


def _mixb(x, a, cc, ga, gb, w_pa, w_pb, w_out, *, tr, tj):
    m, d = x.shape
    dm = w_pa.shape[1]
    assert m % tr == 0 and dm % tj == 0 and tr % ROW_CHUNK == 0
    row = lambda i, j: (i, 0)
    return pl.pallas_call(
        _mixb_body,
        grid=(m // tr, dm // tj),
        in_specs=[
            pl.BlockSpec((tr, d), row),
            pl.BlockSpec((tr // 2, a.shape[1]), row),
            pl.BlockSpec((tr // 2, cc.shape[1]), row),
            pl.BlockSpec((tr, tj), lambda i, j: (i, j)),
            pl.BlockSpec((tr, tj), lambda i, j: (i, j)),
            pl.BlockSpec((w_pa.shape[0], tj), lambda i, j: (0, j)),
            pl.BlockSpec((w_pb.shape[0], tj), lambda i, j: (0, j)),
            pl.BlockSpec((tj, d), lambda i, j: (j, 0)),
        ],
        out_specs=pl.BlockSpec((tr, d), row),
        out_shape=jax.ShapeDtypeStruct((m, d), F32),
        compiler_params=_params(),
        name="mixb",
    )(x, a, cc, ga, gb, w_pa, w_pb, w_out)


def kernel(x_prompt, x_sample, cache_conv, ffn1_norm, ffn1_w_gu, ffn1_w_down, mix_norm, w_in, w_s, b_s, ln_v_g, ln_v_b, w_pa, w_dw, b_dw, ln_c_g, ln_c_b, w_pb, w_out, ffn2_norm, ffn2_w_gu, ffn2_w_down, final_norm):
    n_p, t_p, d = x_prompt.shape
    n_s, t_s, _ = x_sample.shape
    depth, n_head, chunk, _ = w_s.shape
    d_a = ln_v_g.shape[1]
    hd = d_a // n_head
    k_conv = w_dw.shape[1]
    assert t_s <= chunk and k_conv - 1 <= CARRY_ROWS and n_s % (2 * SUBLANES) == 0
    last_len = t_p - chunk * ((t_p - 1) // chunk)
    assert last_len == chunk, "prompt length must be a whole number of chunks"

    xp = x_prompt.reshape(n_p * t_p, d)
    xs = jnp.swapaxes(x_sample, 0, 1).reshape(t_s * n_s, d)
    m_s = t_s * n_s
    row2 = lambda v: v.reshape(1, -1)
    fgain = row2(final_norm)

    tiles = _tile_sizes()
    tiles_per_seq = t_p // tiles.proj_rows
    vp_list, vs_list, cp_list, cs_list = [], [], [], []
    for l in range(depth):
        last = l == depth - 1
        w1gu, w1d = ffn1_w_gu[l].astype(BF16), ffn1_w_down[l].astype(BF16)
        win = w_in[l].astype(BF16)
        g1, gm, g2 = row2(ffn1_norm[l]), row2(mix_norm[l]), row2(ffn2_norm[l])
        bdw = row2(b_dw[l])
        lnvg, lnvb, lncg, lncb = row2(ln_v_g[l]), row2(ln_v_b[l]), row2(ln_c_g[l]), row2(ln_c_b[l])
        bs_slab = jnp.repeat(b_s[l].T, hd, axis=1)
        coef = jnp.repeat(jnp.transpose(w_s[l][:, :t_s, :t_s], (1, 2, 0)), hd, axis=2)
        cache_t = jnp.swapaxes(cache_conv[l], 0, 1)

        xp, hp, w2gu, w2d = _ffn(xp, g1, w1gu, w1d, gm, tm=tiles.ffn_rows,
                                 tf=tiles.ffn_hidden_emit, emit_next_norm=True,
                                 side_cast=(ffn2_w_gu[l], ffn2_w_down[l]))
        u, gv, c, ga, gb, tail, wpa, wpb, wout = _proj_prompt(
            hp, win, w_dw[l], bdw, tm=tiles.proj_rows, tn=tiles.proj_cols, seq_len=t_p,
            side_cast=(w_pa[l], w_pb[l], w_out[l]))
        tail = tail[tiles_per_seq - 1::tiles_per_seq]
        a, cc, v_p = _mixa_prompt(u, gv, c, w_s[l], bs_slab, lnvg, lnvb, lncg, lncb,
                                  tr=tiles.mixa_rows, seq_len=t_p)
        xp = _mixb(xp, a, cc, ga, gb, wpa, wpb, wout, tr=tiles.mix_rows, tj=tiles.mix_cols)
        (xp,) = _ffn(xp, g2, w2gu, w2d, fgain, tm=tiles.ffn_rows, tf=tiles.ffn_hidden,
                     final_norm=last)
        vp_list.append(v_p)
        cp_list.append(tail[:, CARRY_ROWS - (k_conv - 1):, :])

        xs, hs = _ffn(xs, g1, w1gu, w1d, gm, tm=m_s, tf=tiles.ffn_hidden, emit_next_norm=True)
        u, gv, c, ga, gb, g_s = _proj_sample(hs, win, w_dw[l], bdw, cache_t, tn=tiles.proj_cols)
        a, cc, v_s = _mixa_sample(u, gv, c, coef, bs_slab[:t_s], lnvg, lnvb, lncg, lncb, n_seq=n_s)
        xs = _mixb(xs, a, cc, ga, gb, wpa, wpb, wout, tr=m_s, tj=tiles.mix_cols)
        (xs,) = _ffn(xs, g2, w2gu, w2d, fgain, tm=m_s, tf=tiles.ffn_hidden, final_norm=last)
        vs_list.append(jnp.swapaxes(v_s.reshape(t_s, n_s, d_a), 0, 1))
        g_seq = jnp.swapaxes(g_s.reshape(t_s, n_s, -1), 0, 1)
        cs_list.append(jnp.concatenate([cache_conv[l], g_seq], axis=1)[:, -(k_conv - 1):])

    y_prompt = xp.reshape(n_p, t_p, d)
    y_sample = jnp.swapaxes(xs.reshape(t_s, n_s, d), 0, 1)
    return (y_prompt, y_sample, jnp.stack(vp_list), jnp.stack(vs_list),
            jnp.stack(cp_list), jnp.stack(cs_list))
```

```python
import functools
from typing import NamedTuple

import jax
import jax.numpy as jnp
from jax import lax
from jax.experimental import pallas as pl
from jax.experimental.pallas import tpu as pltpu

F32 = jnp.float32
BF16 = jnp.bfloat16
EPS = 1e-6

V7X_VMEM_BYTES = 64 * 1024 * 1024
VMEM_LIMIT_BYTES = V7X_VMEM_BYTES - 4 * 1024 * 1024
SUBLANES = 8
LANES = 128
CONV_STRIDE = 2
CONV_GROUPS = 4
CARRY_ROWS = 32
ROW_CHUNK = 64

SEC_UA, SEC_VA, SEC_GLU_A, SEC_GLU_B, SEC_GATE_A, SEC_GATE_B = range(6)
N_SEC = 6


class _Tiles(NamedTuple):
    ffn_rows: int
    ffn_hidden: int
    ffn_hidden_emit: int
    proj_rows: int
    proj_cols: int
    mixa_rows: int
    mix_rows: int
    mix_cols: int


def _tile_sizes():
    return _Tiles(ffn_rows=1024, ffn_hidden=512, ffn_hidden_emit=256, proj_rows=512, proj_cols=256,
                  mixa_rows=512, mix_rows=1024, mix_cols=256)


def _params(n_axes=2):
    return pltpu.CompilerParams(
        dimension_semantics=("arbitrary",) * n_axes,
        vmem_limit_bytes=VMEM_LIMIT_BYTES,
    )


def _rms(x, gain):
    ms = jnp.mean(x * x, axis=-1, keepdims=True)
    return x * lax.rsqrt(ms + EPS) * gain


def _layer_norm(x, gain, bias):
    mu = jnp.mean(x, axis=-1, keepdims=True)
    xc = x - mu
    var = jnp.mean(xc * xc, axis=-1, keepdims=True)
    return xc * lax.rsqrt(var + EPS) * gain + bias


def _dot(a, b):
    return jnp.dot(a, b, preferred_element_type=F32)


def _pack_rows(x):
    return pltpu.bitcast(x.astype(BF16), jnp.uint32)


def _unpack_rows(packed):
    return pltpu.bitcast(packed, BF16)


def _for_each_row_chunk(n_rows, fn):
    half = ROW_CHUNK // 2

    def body(c, carry):
        fn(pl.ds(pl.multiple_of(c * ROW_CHUNK, ROW_CHUNK), ROW_CHUNK),
           pl.ds(pl.multiple_of(c * half, half), half))
        return carry

    lax.fori_loop(0, n_rows // ROW_CHUNK, body, 0, unroll=2)


def _ffn_body(x_ref, gain_ref, wg_ref, wu_ref, wd_ref, post_gain_ref, *rest,
              final_norm, emit_next_norm, n_side):
    side_in = rest[:n_side]
    o_ref = rest[n_side]
    rest = rest[n_side + 1:]
    if emit_next_norm:
        next_ref, rest = rest[0], rest[1:]
    side_out, (xn_ref,) = rest[:n_side], rest[n_side:]
    k = pl.program_id(1)
    tm = x_ref.shape[0]

    @pl.when(k == 0)
    def _():
        def norm_chunk(rows, _):
            xn_ref[rows, :] = _rms(x_ref[rows, :], gain_ref[...]).astype(BF16)
            o_ref[rows, :] = jnp.zeros((ROW_CHUNK, o_ref.shape[1]), F32)

        _for_each_row_chunk(tm, norm_chunk)

    xn = xn_ref[...]
    gate = _dot(xn, wg_ref[...])
    up = _dot(xn, wu_ref[...])
    act = (jax.nn.silu(gate) * up).astype(BF16)
    for src_ref, dst_ref in zip(side_in, side_out):
        dst_ref[...] = src_ref[...].astype(BF16)
    o_ref[...] += _dot(act, wd_ref[...])

    @pl.when(k == pl.num_programs(1) - 1)
    def _():
        def finish_chunk(rows, packed_rows):
            y = x_ref[rows, :] + 0.5 * o_ref[rows, :]
            if final_norm:
                y = _rms(y, post_gain_ref[...])
            o_ref[rows, :] = y
            if emit_next_norm:
                next_ref[packed_rows, :] = _pack_rows(_rms(y, post_gain_ref[...]))

        _for_each_row_chunk(tm, finish_chunk)


def _side_block(shape, max_blocks):
    r, c = shape
    for n_blocks in range(max_blocks, 0, -1):
        for n_col_blocks in range(1, n_blocks + 1):
            if n_blocks % n_col_blocks or c % n_col_blocks or r % (n_blocks // n_col_blocks):
                continue
            br, bc = r // (n_blocks // n_col_blocks), c // n_col_blocks
            if br % (2 * SUBLANES) == 0 and bc % LANES == 0:
                return br, bc
    raise ValueError(f"cannot cut {shape} into aligned blocks")


def _side_spec(shape, max_blocks, step_of):
    br, bc = _side_block(shape, max_blocks)
    n_col_blocks = shape[1] // bc
    n_blocks = (shape[0] // br) * n_col_blocks

    def index_map(*grid_idx):
        t = jnp.minimum(step_of(*grid_idx), n_blocks - 1)
        return t // n_col_blocks, t % n_col_blocks

    return pl.BlockSpec((br, bc), index_map)


def _ffn(x, gain, w_gu, w_down, post_gain, *, tm, tf, final_norm=False, emit_next_norm=False,
         side_cast=()):
    m, d = x.shape
    f = w_down.shape[0]
    nf = f // tf
    n_steps = (m // tm) * nf
    assert m % tm == 0 and f % tf == 0 and tm % ROW_CHUNK == 0
    assert not (final_norm and emit_next_norm)
    rows = lambda i, k: (i, 0)
    out_specs = [pl.BlockSpec((tm, d), rows)]
    out_shape = [jax.ShapeDtypeStruct((m, d), F32)]
    if emit_next_norm:
        out_specs.append(pl.BlockSpec((tm // 2, d), rows))
        out_shape.append(jax.ShapeDtypeStruct((m // 2, d), jnp.uint32))
    side_specs = []
    for w in side_cast:
        spec = _side_spec(w.shape, n_steps, lambda i, k: i * nf + k)
        side_specs.append(spec)
        out_specs.append(spec)
        out_shape.append(jax.ShapeDtypeStruct(w.shape, BF16))
    return pl.pallas_call(
        functools.partial(_ffn_body, final_norm=final_norm, emit_next_norm=emit_next_norm,
                          n_side=len(side_cast)),
        grid=(m // tm, nf),
        in_specs=[
            pl.BlockSpec((tm, d), rows),
            pl.BlockSpec((1, d), lambda i, k: (0, 0)),
            pl.BlockSpec((d, tf), lambda i, k: (0, k)),
            pl.BlockSpec((d, tf), lambda i, k: (0, k + nf)),
            pl.BlockSpec((tf, d), lambda i, k: (k, 0)),
            pl.BlockSpec((1, d), lambda i, k: (0, 0)),
            *side_specs,
        ],
        out_specs=out_specs,
        out_shape=out_shape,
        scratch_shapes=[pltpu.VMEM((tm, d), BF16)],
        compiler_params=_params(),
        name="ffn",
    )(x, gain, w_gu, w_gu, w_down, post_gain, *side_cast)


def _section_outputs(pre, u_ref, gv_ref, ga_ref, gb_ref):
    u_ref[...] = jax.nn.gelu(pre(SEC_UA)).astype(BF16)
    gv_ref[...] = jax.nn.gelu(pre(SEC_VA))
    ga_ref[...] = jax.nn.sigmoid(pre(SEC_GATE_A)).astype(BF16)
    gb_ref[...] = jax.nn.sigmoid(pre(SEC_GATE_B)).astype(BF16)
    return pre(SEC_GLU_A) * jax.nn.sigmoid(pre(SEC_GLU_B))


def _causal_conv(gext_ref, cout_ref, wdw_ref, bdw_ref, c_ref):
    n_slab, tm, _ = cout_ref.shape
    k_conv = wdw_ref.shape[0]
    shift = CARRY_ROWS - (k_conv - 1)
    group = CONV_STRIDE * SUBLANES
    starts = [b * group + p for b in range(CONV_GROUPS) for p in range(CONV_STRIDE)]
    for s in range(n_slab):
        lanes = slice(s * LANES, (s + 1) * LANES)
        bias = jnp.broadcast_to(bdw_ref[:, lanes], (SUBLANES, LANES))
        for r0 in range(0, tm, CONV_GROUPS * group):
            accs = [bias for _ in starts]
            for k in range(k_conv):
                wk = jnp.broadcast_to(wdw_ref[k:k + 1, lanes], (SUBLANES, LANES))
                for q, st in enumerate(starts):
                    rows = pl.ds(r0 + st + shift + k, SUBLANES, stride=CONV_STRIDE)
                    accs[q] = accs[q] + wk * gext_ref[s, rows, :]
            for q, st in enumerate(starts):
                cout_ref[s, pl.ds(r0 + st, SUBLANES, stride=CONV_STRIDE), :] = accs[q]
        c_ref[:, lanes] = cout_ref[s]


def _proj_prompt_body(h_ref, w0, w1, w2, w3, w4, w5, wdw_ref, bdw_ref, *rest,
                      n_steps, tiles_per_seq, n_side):
    side_in, rest = rest[:n_side], rest[n_side:]
    u_ref, gv_ref, c_ref, ga_ref, gb_ref, tail_ref = rest[:6]
    side_out, rest = rest[6:6 + n_side], rest[6 + n_side:]
    raw_even_ref, raw_odd_ref, gext_ref, cout_ref, carry_ref = rest
    s = pl.program_id(0)
    n_slab, _, _ = cout_ref.shape
    tm = cout_ref.shape[1]
    w_refs = (w0, w1, w2, w3, w4, w5)
    sub = jnp.minimum(s, n_steps - 1) % tiles_per_seq
    prev_sub = jnp.maximum(s - 1, 0) % tiles_per_seq

    @pl.when(s == 0)
    def _():
        raw_odd_ref[...] = jnp.zeros_like(raw_odd_ref)

    @pl.when(prev_sub == 0)
    def _():
        carry_ref[...] = jnp.zeros_like(carry_ref)

    def step(raw_w, raw_r):
        for src_ref, dst_ref in zip(side_in, side_out):
            dst_ref[...] = src_ref[...].astype(BF16)
        packed_rows = pl.ds(pl.multiple_of(sub * (tm // 2), tm // 2), tm // 2)
        h = _unpack_rows(h_ref[packed_rows, :])
        g = _section_outputs(lambda sec: raw_r[sec], u_ref, gv_ref, ga_ref, gb_ref)
        gext_ref[:, 0:CARRY_ROWS, :] = carry_ref[...]
        for sl in range(n_slab):
            gext_ref[sl, CARRY_ROWS:CARRY_ROWS + tm, :] = g[:, sl * LANES:(sl + 1) * LANES]
            carry_ref[sl] = g[tm - CARRY_ROWS:, sl * LANES:(sl + 1) * LANES]
        tail_ref[0] = g[tm - CARRY_ROWS:, :]
        _causal_conv(gext_ref, cout_ref, wdw_ref, bdw_ref, c_ref)

        for sec in range(N_SEC):
            raw_w[sec] = _dot(h, w_refs[sec][...])

    @pl.when(s % 2 == 0)
    def _():
        step(raw_even_ref, raw_odd_ref)

    @pl.when(s % 2 == 1)
    def _():
        step(raw_odd_ref, raw_even_ref)


def _proj_sample_body(h_ref, w0, w1, w2, w3, w4, w5, wdw_ref, bdw_ref, cache_ref,
                      u_ref, gv_ref, c_ref, ga_ref, gb_ref, g_ref, *, n_seq):
    w_refs = (w0, w1, w2, w3, w4, w5)
    k_conv = wdw_ref.shape[0]
    n_prev = k_conv - 1
    t_new = 2 * h_ref.shape[0] // n_seq

    h = _unpack_rows(h_ref[...])
    g_ref[...] = _section_outputs(lambda sec: _dot(h, w_refs[sec][...]),
                                  u_ref, gv_ref, ga_ref, gb_ref)

    bias = bdw_ref[...]
    for t in range(t_new):
        acc = jnp.broadcast_to(bias, (n_seq, bias.shape[1]))
        for k in range(k_conv):
            m = t + k
            if m < n_prev:
                slab = cache_ref[m]
            else:
                slab = g_ref[(m - n_prev) * n_seq:(m - n_prev + 1) * n_seq, :]
            acc = acc + wdw_ref[k:k + 1, :] * slab
        c_ref[t * n_seq:(t + 1) * n_seq, :] = acc


def _proj_out_shapes(m, dsec):
    return [
        jax.ShapeDtypeStruct((m, dsec), BF16),
        jax.ShapeDtypeStruct((m, dsec), F32),
        jax.ShapeDtypeStruct((m, dsec), F32),
        jax.ShapeDtypeStruct((m, dsec), BF16),
        jax.ShapeDtypeStruct((m, dsec), BF16),
    ]


def _proj_prompt(h, w_in, w_dw, b_dw, *, tm, tn, seq_len, side_cast=()):
    m, d = 2 * h.shape[0], h.shape[1]
    dsec = w_in.shape[1] // N_SEC
    nblk = dsec // tn
    n_slab = tn // LANES
    k_conv = w_dw.shape[0]
    assert m % tm == 0 and dsec % tn == 0 and tn % LANES == 0
    assert seq_len % tm == 0 and tm >= CARRY_ROWS
    assert tm % (CONV_GROUPS * CONV_STRIDE * SUBLANES) == 0
    tps = seq_len // tm
    n_steps = (m // tm) * nblk
    seq_of = lambda t: t // (nblk * tps)
    col_of = lambda t: (t // tps) % nblk
    row_tile_of = lambda t: seq_of(t) * tps + t % tps
    cur = lambda s: jnp.minimum(s, n_steps - 1)
    prev = lambda s: jnp.maximum(s - 1, 0)
    w_specs = [pl.BlockSpec((d, tn), lambda s, sec=sec: (0, sec * nblk + col_of(cur(s))))
               for sec in range(N_SEC)]
    out_tile = lambda s: (row_tile_of(prev(s)), col_of(prev(s)))
    side_specs, side_shapes = [], []
    for w in side_cast:
        side_specs.append(_side_spec(w.shape, n_steps, lambda s: s))
        side_shapes.append(jax.ShapeDtypeStruct(w.shape, BF16))
    return pl.pallas_call(
        functools.partial(_proj_prompt_body, n_steps=n_steps, tiles_per_seq=tps,
                          n_side=len(side_cast)),
        grid=(n_steps + 1,),
        in_specs=[
            pl.BlockSpec((seq_len // 2, d), lambda s: (seq_of(cur(s)), 0)),
            *w_specs,
            pl.BlockSpec((k_conv, tn), lambda s: (0, col_of(prev(s)))),
            pl.BlockSpec((1, tn), lambda s: (0, col_of(prev(s)))),
            *side_specs,
        ],
        out_specs=[pl.BlockSpec((tm, tn), out_tile) for _ in range(5)]
        + [pl.BlockSpec((1, CARRY_ROWS, tn),
                        lambda s: (row_tile_of(prev(s)), 0, col_of(prev(s))))]
        + side_specs,
        out_shape=_proj_out_shapes(m, dsec)
        + [jax.ShapeDtypeStruct((m // tm, CARRY_ROWS, dsec), F32)] + side_shapes,
        scratch_shapes=[
            pltpu.VMEM((N_SEC, tm, tn), F32),
            pltpu.VMEM((N_SEC, tm, tn), F32),
            pltpu.VMEM((n_slab, CARRY_ROWS + tm, LANES), F32),
            pltpu.VMEM((n_slab, tm, LANES), F32),
            pltpu.VMEM((n_slab, CARRY_ROWS, LANES), F32),
        ],
        compiler_params=_params(1),
        name="proj_prompt",
    )(h, *([w_in] * N_SEC), w_dw, b_dw, *side_cast)


def _proj_sample(h, w_in, w_dw, b_dw, cache_t, *, tn):
    m, d = 2 * h.shape[0], h.shape[1]
    dsec = w_in.shape[1] // N_SEC
    nblk = dsec // tn
    k_conv = w_dw.shape[0]
    n_seq = cache_t.shape[1]
    assert dsec % tn == 0 and m % n_seq == 0
    w_specs = [pl.BlockSpec((d, tn), lambda j, sec=sec: (0, sec * nblk + j))
               for sec in range(N_SEC)]
    return pl.pallas_call(
        functools.partial(_proj_sample_body, n_seq=n_seq),
        grid=(nblk,),
        in_specs=[
            pl.BlockSpec((m // 2, d), lambda j: (0, 0)),
            *w_specs,
            pl.BlockSpec((k_conv, tn), lambda j: (0, j)),
            pl.BlockSpec((1, tn), lambda j: (0, j)),
            pl.BlockSpec((k_conv - 1, n_seq, tn), lambda j: (0, 0, j)),
        ],
        out_specs=[pl.BlockSpec((m, tn), lambda j: (0, j)) for _ in range(6)],
        out_shape=_proj_out_shapes(m, dsec) + [jax.ShapeDtypeStruct((m, dsec), F32)],
        compiler_params=_params(1),
        name="proj_sample",
    )(h, *([w_in] * N_SEC), w_dw, b_dw, cache_t)


def _mixa_prompt_body(u_ref, gv_ref, c_ref, ws_ref, bs_ref, lnvg_ref, lnvb_ref, lncg_ref, lncb_ref,
                      a_ref, cc_ref, vst_ref, vb_ref):
    tr = u_ref.shape[0]
    n_head, chunk, _ = ws_ref.shape
    hd = u_ref.shape[1] // n_head
    for r in range(0, tr, ROW_CHUNK):
        rows = slice(r, r + ROW_CHUNK)
        v = _layer_norm(gv_ref[rows, :], lnvg_ref[...], lnvb_ref[...])
        if r >= tr - chunk:
            vst_ref[0, r - (tr - chunk):r - (tr - chunk) + ROW_CHUNK, :] = v
        vb_ref[rows, :] = v.astype(BF16)
        cc_ref[r // 2:(r + ROW_CHUNK) // 2, :] = _pack_rows(jax.nn.silu(
            _layer_norm(c_ref[rows, :], lncg_ref[...], lncb_ref[...])))
    row = lax.broadcasted_iota(jnp.int32, (chunk, chunk), 0)
    col = lax.broadcasted_iota(jnp.int32, (chunk, chunk), 1)
    causal = (row >= col).astype(F32)
    for h in range(n_head):
        w = (ws_ref[h] * causal).astype(BF16)
        cols = slice(h * hd, (h + 1) * hd)
        for c in range(tr // chunk):
            rows = slice(c * chunk, (c + 1) * chunk)
            s = _dot(w, vb_ref[rows, cols]) + bs_ref[:, cols]
            packed_rows = slice(c * chunk // 2, (c + 1) * chunk // 2)
            a_ref[packed_rows, cols] = _pack_rows(u_ref[rows, cols].astype(F32) * s)


def _mixa_sample_body(u_ref, gv_ref, c_ref, coef_ref, bs_ref, lnvg_ref, lnvb_ref, lncg_ref,
                      lncb_ref, a_ref, cc_ref, v_ref, *, n_seq):
    t_new = u_ref.shape[0] // n_seq
    v_ref[...] = _layer_norm(gv_ref[...], lnvg_ref[...], lnvb_ref[...])
    for i in range(t_new):
        rows_i = slice(i * n_seq, (i + 1) * n_seq)
        s = jnp.broadcast_to(bs_ref[i:i + 1, :], (n_seq, bs_ref.shape[1]))
        for j in range(i + 1):
            s = s + coef_ref[i, j:j + 1, :] * v_ref[j * n_seq:(j + 1) * n_seq, :]
        a_ref[i * n_seq // 2:(i + 1) * n_seq // 2, :] = _pack_rows(
            u_ref[rows_i, :].astype(F32) * s)
    cc_ref[...] = _pack_rows(
        jax.nn.silu(_layer_norm(c_ref[...], lncg_ref[...], lncb_ref[...])))


def _mixa_prompt(u, gv, c, w_s, bs_slab, ln_v_g, ln_v_b, ln_c_g, ln_c_b, *, tr, seq_len):
    m, d = u.shape
    n_head, chunk, _ = w_s.shape
    assert seq_len % tr == 0 and tr % chunk == 0 and chunk % ROW_CHUNK == 0
    tiles_per_seq = seq_len // tr
    row = lambda i: (i, 0)
    const2 = lambda i: (0, 0)
    return pl.pallas_call(
        _mixa_prompt_body,
        grid=(m // tr,),
        in_specs=[
            pl.BlockSpec((tr, d), row),
            pl.BlockSpec((tr, d), row),
            pl.BlockSpec((tr, d), row),
            pl.BlockSpec((n_head, chunk, chunk), lambda i: (0, 0, 0)),
            pl.BlockSpec((chunk, d), const2),
            pl.BlockSpec((1, d), const2),
            pl.BlockSpec((1, d), const2),
            pl.BlockSpec((1, d), const2),
            pl.BlockSpec((1, d), const2),
        ],
        out_specs=[
            pl.BlockSpec((tr // 2, d), row),
            pl.BlockSpec((tr // 2, d), row),
            pl.BlockSpec((1, chunk, d), lambda i: (i // tiles_per_seq, 0, 0)),
        ],
        out_shape=[
            jax.ShapeDtypeStruct((m // 2, d), jnp.uint32),
            jax.ShapeDtypeStruct((m // 2, d), jnp.uint32),
            jax.ShapeDtypeStruct((m // seq_len, chunk, d), F32),
        ],
        scratch_shapes=[pltpu.VMEM((tr, d), BF16)],
        compiler_params=_params(1),
        name="mixa_prompt",
    )(u, gv, c, w_s, bs_slab, ln_v_g, ln_v_b, ln_c_g, ln_c_b)


def _mixa_sample(u, gv, c, coef, bs_rows, ln_v_g, ln_v_b, ln_c_g, ln_c_b, *, n_seq):
    m, d = u.shape
    t_new = m // n_seq
    whole = lambda i: (0, 0)
    return pl.pallas_call(
        functools.partial(_mixa_sample_body, n_seq=n_seq),
        grid=(1,),
        in_specs=[
            pl.BlockSpec((m, d), whole),
            pl.BlockSpec((m, d), whole),
            pl.BlockSpec((m, d), whole),
            pl.BlockSpec((t_new, t_new, d), lambda i: (0, 0, 0)),
            pl.BlockSpec((t_new, d), whole),
            pl.BlockSpec((1, d), whole),
            pl.BlockSpec((1, d), whole),
            pl.BlockSpec((1, d), whole),
            pl.BlockSpec((1, d), whole),
        ],
        out_specs=[pl.BlockSpec((m // 2, d), whole), pl.BlockSpec((m // 2, d), whole),
                   pl.BlockSpec((m, d), whole)],
        out_shape=[
            jax.ShapeDtypeStruct((m // 2, d), jnp.uint32),
            jax.ShapeDtypeStruct((m // 2, d), jnp.uint32),
            jax.ShapeDtypeStruct((m, d), F32),
        ],
        compiler_params=_params(1),
        name="mixa_sample",
    )(u, gv, c, coef, bs_rows, ln_v_g, ln_v_b, ln_c_g, ln_c_b)


def _mixb_body(x_ref, a_ref, cc_ref, ga_ref, gb_ref, wpa_ref, wpb_ref, wout_ref, o_ref):
    j = pl.program_id(1)

    @pl.when(j == 0)
    def _():
        o_ref[...] = jnp.zeros_like(o_ref)

    ya = _dot(_unpack_rows(a_ref[...]), wpa_ref[...])
    yb = _dot(_unpack_rows(cc_ref[...]), wpb_ref[...])
    merged = (ga_ref[...].astype(F32) * ya + gb_ref[...].astype(F32) * yb).astype(BF16)
    o_ref[...] += _dot(merged, wout_ref[...])

    @pl.when(j == pl.num_programs(1) - 1)
    def _():
        def add_chunk(rows, _):
            o_ref[rows, :] = x_ref[rows, :] + o_ref[rows, :]

        _for_each_row_chunk(x_ref.shape[0], add_chunk)


def _mixb(x, a, cc, ga, gb, w_pa, w_pb, w_out, *, tr, tj):
    m, d = x.shape
    dm = w_pa.shape[1]
    assert m % tr == 0 and dm % tj == 0 and tr % ROW_CHUNK == 0
    nj = dm // tj
    row = lambda i, j: (i, 0)
    x_row = lambda i, j: (jnp.where(j >= nj // 2, i, jnp.maximum(i - 1, 0)), 0)
    return pl.pallas_call(
        _mixb_body,
        grid=(m // tr, nj),
        in_specs=[
            pl.BlockSpec((tr, d), x_row),
            pl.BlockSpec((tr // 2, a.shape[1]), row),
            pl.BlockSpec((tr // 2, cc.shape[1]), row),
            pl.BlockSpec((tr, tj), lambda i, j: (i, j)),
            pl.BlockSpec((tr, tj), lambda i, j: (i, j)),
            pl.BlockSpec((w_pa.shape[0], tj), lambda i, j: (0, j)),
            pl.BlockSpec((w_pb.shape[0], tj), lambda i, j: (0, j)),
            pl.BlockSpec((tj, d), lambda i, j: (j, 0)),
        ],
        out_specs=pl.BlockSpec((tr, d), row),
        out_shape=jax.ShapeDtypeStruct((m, d), F32),
        compiler_params=_params(),
        name="mixb",
    )(x, a, cc, ga, gb, w_pa, w_pb, w_out)


def kernel(x_prompt, x_sample, cache_conv, ffn1_norm, ffn1_w_gu, ffn1_w_down, mix_norm, w_in, w_s, b_s, ln_v_g, ln_v_b, w_pa, w_dw, b_dw, ln_c_g, ln_c_b, w_pb, w_out, ffn2_norm, ffn2_w_gu, ffn2_w_down, final_norm):
    n_p, t_p, d = x_prompt.shape
    n_s, t_s, _ = x_sample.shape
    depth, n_head, chunk, _ = w_s.shape
    d_a = ln_v_g.shape[1]
    hd = d_a // n_head
    k_conv = w_dw.shape[1]
    assert t_s <= chunk and k_conv - 1 <= CARRY_ROWS and n_s % (2 * SUBLANES) == 0
    last_len = t_p - chunk * ((t_p - 1) // chunk)
    assert last_len == chunk, "prompt length must be a whole number of chunks"

    xp = x_prompt.reshape(n_p * t_p, d)
    xs = jnp.swapaxes(x_sample, 0, 1).reshape(t_s * n_s, d)
    m_s = t_s * n_s
    row2 = lambda v: v.reshape(1, -1)
    fgain = row2(final_norm)

    tiles = _tile_sizes()
    tiles_per_seq = t_p // tiles.proj_rows
    vp_list, vs_list, cp_list, cs_list = [], [], [], []
    for l in range(depth):
        last = l == depth - 1
        w1gu, w1d = ffn1_w_gu[l].astype(BF16), ffn1_w_down[l].astype(BF16)
        g1, gm, g2 = row2(ffn1_norm[l]), row2(mix_norm[l]), row2(ffn2_norm[l])
        bdw = row2(b_dw[l])
        lnvg, lnvb, lncg, lncb = row2(ln_v_g[l]), row2(ln_v_b[l]), row2(ln_c_g[l]), row2(ln_c_b[l])
        bs_slab = jnp.repeat(b_s[l].T, hd, axis=1)
        coef = jnp.repeat(jnp.transpose(w_s[l][:, :t_s, :t_s], (1, 2, 0)), hd, axis=2)
        cache_t = jnp.swapaxes(cache_conv[l], 0, 1)

        xp, hp, win, w2gu = _ffn(xp, g1, w1gu, w1d, gm, tm=tiles.ffn_rows,
                                 tf=tiles.ffn_hidden_emit, emit_next_norm=True,
                                 side_cast=(w_in[l], ffn2_w_gu[l]))
        u, gv, c, ga, gb, tail, w2d, wpa, wpb, wout = _proj_prompt(
            hp, win, w_dw[l], bdw, tm=tiles.proj_rows, tn=tiles.proj_cols, seq_len=t_p,
            side_cast=(ffn2_w_down[l], w_pa[l], w_pb[l], w_out[l]))
        tail = tail[tiles_per_seq - 1::tiles_per_seq]
        a, cc, v_p = _mixa_prompt(u, gv, c, w_s[l], bs_slab, lnvg, lnvb, lncg, lncb,
                                  tr=tiles.mixa_rows, seq_len=t_p)
        xp = _mixb(xp, a, cc, ga, gb, wpa, wpb, wout, tr=tiles.mix_rows, tj=tiles.mix_cols)
        (xp,) = _ffn(xp, g2, w2gu, w2d, fgain, tm=tiles.ffn_rows, tf=tiles.ffn_hidden,
                     final_norm=last)
        vp_list.append(v_p)
        cp_list.append(tail[:, CARRY_ROWS - (k_conv - 1):, :])

        xs, hs = _ffn(xs, g1, w1gu, w1d, gm, tm=m_s, tf=tiles.ffn_hidden, emit_next_norm=True)
        u, gv, c, ga, gb, g_s = _proj_sample(hs, win, w_dw[l], bdw, cache_t, tn=tiles.proj_cols)
        a, cc, v_s = _mixa_sample(u, gv, c, coef, bs_slab[:t_s], lnvg, lnvb, lncg, lncb, n_seq=n_s)
        xs = _mixb(xs, a, cc, ga, gb, wpa, wpb, wout, tr=m_s, tj=tiles.mix_cols)
        (xs,) = _ffn(xs, g2, w2gu, w2d, fgain, tm=m_s, tf=tiles.ffn_hidden, final_norm=last)
        vs_list.append(jnp.swapaxes(v_s.reshape(t_s, n_s, d_a), 0, 1))
        g_seq = jnp.swapaxes(g_s.reshape(t_s, n_s, -1), 0, 1)
        cs_list.append(jnp.concatenate([cache_conv[l], g_seq], axis=1)[:, -(k_conv - 1):])

    y_prompt = xp.reshape(n_p, t_p, d)
    y_sample = jnp.swapaxes(xs.reshape(t_s, n_s, d), 0, 1)
    return (y_prompt, y_sample, jnp.stack(vp_list), jnp.stack(vs_list),
            jnp.stack(cp_list), jnp.stack(cs_list))
```

```python
import functools
from typing import NamedTuple

import jax
import jax.numpy as jnp
from jax import lax
from jax.experimental import pallas as pl
from jax.experimental.pallas import tpu as pltpu

F32 = jnp.float32
BF16 = jnp.bfloat16
EPS = 1e-6

V7X_VMEM_BYTES = 64 * 1024 * 1024
VMEM_LIMIT_BYTES = V7X_VMEM_BYTES - 4 * 1024 * 1024
SUBLANES = 8
LANES = 128
CONV_STRIDE = 2
CONV_GROUPS = 4
CARRY_ROWS = 32
ROW_CHUNK = 64

SEC_UA, SEC_VA, SEC_GLU_A, SEC_GLU_B, SEC_GATE_A, SEC_GATE_B = range(6)
N_SEC = 6


class _Tiles(NamedTuple):
    ffn_rows: int
    ffn_hidden: int
    ffn_hidden_emit: int
    proj_rows: int
    proj_cols: int
    mixa_rows: int
    mix_rows: int
    mix_cols: int


def _tile_sizes():
    return _Tiles(ffn_rows=1024, ffn_hidden=512, ffn_hidden_emit=256, proj_rows=512, proj_cols=256,
                  mixa_rows=512, mix_rows=1024, mix_cols=256)


def _params(n_axes=2):
    return pltpu.CompilerParams(
        dimension_semantics=("arbitrary",) * n_axes,
        vmem_limit_bytes=VMEM_LIMIT_BYTES,
    )


def _rms(x, gain):
    ms = jnp.mean(x * x, axis=-1, keepdims=True)
    return x * lax.rsqrt(ms + EPS) * gain


def _layer_norm(x, gain, bias):
    mu = jnp.mean(x, axis=-1, keepdims=True)
    xc = x - mu
    var = jnp.mean(xc * xc, axis=-1, keepdims=True)
    return xc * lax.rsqrt(var + EPS) * gain + bias


def _dot(a, b):
    return jnp.dot(a, b, preferred_element_type=F32)


def _pack_rows(x):
    return pltpu.bitcast(x.astype(BF16), jnp.uint32)


def _unpack_rows(packed):
    return pltpu.bitcast(packed, BF16)


def _for_each_row_chunk(n_rows, fn):
    half = ROW_CHUNK // 2

    def body(c, carry):
        fn(pl.ds(pl.multiple_of(c * ROW_CHUNK, ROW_CHUNK), ROW_CHUNK),
           pl.ds(pl.multiple_of(c * half, half), half))
        return carry

    lax.fori_loop(0, n_rows // ROW_CHUNK, body, 0, unroll=2)


def _ffn_body(x_ref, gain_ref, wg_ref, wu_ref, wd_ref, post_gain_ref, *rest,
              final_norm, emit_next_norm, n_side):
    side_in = rest[:n_side]
    o_ref = rest[n_side]
    rest = rest[n_side + 1:]
    if emit_next_norm:
        next_ref, rest = rest[0], rest[1:]
    side_out, (xn_ref,) = rest[:n_side], rest[n_side:]
    k = pl.program_id(1)
    tm = x_ref.shape[0]

    @pl.when(k == 0)
    def _():
        def norm_chunk(rows, _):
            x = x_ref[rows, :]
            xn_ref[rows, :] = _rms(x, gain_ref[...]).astype(BF16)
            o_ref[rows, :] = x

        _for_each_row_chunk(tm, norm_chunk)

    xn = xn_ref[...]
    gate = _dot(xn, wg_ref[...])
    up = _dot(xn, wu_ref[...])
    act = (jax.nn.silu(gate) * (0.5 * up)).astype(BF16)
    for src_ref, dst_ref in zip(side_in, side_out):
        dst_ref[...] = src_ref[...].astype(BF16)
    o_ref[...] += _dot(act, wd_ref[...])

    if final_norm or emit_next_norm:
        @pl.when(k == pl.num_programs(1) - 1)
        def _():
            def finish_chunk(rows, packed_rows):
                y = o_ref[rows, :]
                if final_norm:
                    o_ref[rows, :] = _rms(y, post_gain_ref[...])
                if emit_next_norm:
                    next_ref[packed_rows, :] = _pack_rows(_rms(y, post_gain_ref[...]))

            _for_each_row_chunk(tm, finish_chunk)


def _side_block(shape, max_blocks):
    r, c = shape
    for n_blocks in range(max_blocks, 0, -1):
        for n_col_blocks in range(1, n_blocks + 1):
            if n_blocks % n_col_blocks or c % n_col_blocks or r % (n_blocks // n_col_blocks):
                continue
            br, bc = r // (n_blocks // n_col_blocks), c // n_col_blocks
            if br % (2 * SUBLANES) == 0 and bc % LANES == 0:
                return br, bc
    raise ValueError(f"cannot cut {shape} into aligned blocks")


def _side_spec(shape, max_blocks, step_of):
    br, bc = _side_block(shape, max_blocks)
    n_col_blocks = shape[1] // bc
    n_blocks = (shape[0] // br) * n_col_blocks

    def index_map(*grid_idx):
        t = jnp.minimum(step_of(*grid_idx), n_blocks - 1)
        return t // n_col_blocks, t % n_col_blocks

    return pl.BlockSpec((br, bc), index_map)


def _ffn(x, gain, w_gu, w_down, post_gain, *, tm, tf, final_norm=False, emit_next_norm=False,
         side_cast=()):
    m, d = x.shape
    f = w_down.shape[0]
    nf = f // tf
    n_steps = (m // tm) * nf
    assert m % tm == 0 and f % tf == 0 and tm % ROW_CHUNK == 0
    assert not (final_norm and emit_next_norm)
    rows = lambda i, k: (i, 0)
    last_tile = m // tm - 1
    x_rows = lambda i, k: (jnp.where(k >= 1, jnp.minimum(i + 1, last_tile), i), 0)
    out_specs = [pl.BlockSpec((tm, d), rows)]
    out_shape = [jax.ShapeDtypeStruct((m, d), F32)]
    if emit_next_norm:
        out_specs.append(pl.BlockSpec((tm // 2, d), rows))
        out_shape.append(jax.ShapeDtypeStruct((m // 2, d), jnp.uint32))
    side_specs = []
    for w in side_cast:
        spec = _side_spec(w.shape, n_steps, lambda i, k: i * nf + k)
        side_specs.append(spec)
        out_specs.append(spec)
        out_shape.append(jax.ShapeDtypeStruct(w.shape, BF16))
    return pl.pallas_call(
        functools.partial(_ffn_body, final_norm=final_norm, emit_next_norm=emit_next_norm,
                          n_side=len(side_cast)),
        grid=(m // tm, nf),
        in_specs=[
            pl.BlockSpec((tm, d), x_rows),
            pl.BlockSpec((1, d), lambda i, k: (0, 0)),
            pl.BlockSpec((d, tf), lambda i, k: (0, k)),
            pl.BlockSpec((d, tf), lambda i, k: (0, k + nf)),
            pl.BlockSpec((tf, d), lambda i, k: (k, 0)),
            pl.BlockSpec((1, d), lambda i, k: (0, 0)),
            *side_specs,
        ],
        out_specs=out_specs,
        out_shape=out_shape,
        scratch_shapes=[pltpu.VMEM((tm, d), BF16)],
        compiler_params=_params(),
        name="ffn",
    )(x, gain, w_gu, w_gu, w_down, post_gain, *side_cast)


def _section_outputs(pre, u_ref, gv_ref, ga_ref, gb_ref):
    u_ref[...] = jax.nn.gelu(pre(SEC_UA)).astype(BF16)
    gv_ref[...] = jax.nn.gelu(pre(SEC_VA))
    ga_ref[...] = jax.nn.sigmoid(pre(SEC_GATE_A)).astype(BF16)
    gb_ref[...] = jax.nn.sigmoid(pre(SEC_GATE_B)).astype(BF16)
    return pre(SEC_GLU_A) * jax.nn.sigmoid(pre(SEC_GLU_B))


def _causal_conv(gext_ref, cout_ref, wdw_ref, bdw_ref, c_ref):
    n_slab, tm, _ = cout_ref.shape
    k_conv = wdw_ref.shape[0]
    shift = CARRY_ROWS - (k_conv - 1)
    group = CONV_STRIDE * SUBLANES
    starts = [b * group + p for b in range(CONV_GROUPS) for p in range(CONV_STRIDE)]
    for s in range(n_slab):
        lanes = slice(s * LANES, (s + 1) * LANES)
        bias = jnp.broadcast_to(bdw_ref[:, lanes], (SUBLANES, LANES))
        for r0 in range(0, tm, CONV_GROUPS * group):
            accs = [bias for _ in starts]
            for k in range(k_conv):
                wk = jnp.broadcast_to(wdw_ref[k:k + 1, lanes], (SUBLANES, LANES))
                for q, st in enumerate(starts):
                    rows = pl.ds(r0 + st + shift + k, SUBLANES, stride=CONV_STRIDE)
                    accs[q] = accs[q] + wk * gext_ref[s, rows, :]
            for q, st in enumerate(starts):
                cout_ref[s, pl.ds(r0 + st, SUBLANES, stride=CONV_STRIDE), :] = accs[q]
        c_ref[:, lanes] = cout_ref[s]


def _proj_prompt_body(h_ref, w0, w1, w2, w3, w4, w5, wdw_ref, bdw_ref, *rest,
                      n_steps, tiles_per_seq, n_side):
    side_in, rest = rest[:n_side], rest[n_side:]
    u_ref, gv_ref, c_ref, ga_ref, gb_ref, tail_ref = rest[:6]
    side_out, rest = rest[6:6 + n_side], rest[6 + n_side:]
    raw_even_ref, raw_odd_ref, gext_ref, cout_ref, carry_ref = rest
    s = pl.program_id(0)
    n_slab, _, _ = cout_ref.shape
    tm = cout_ref.shape[1]
    w_refs = (w0, w1, w2, w3, w4, w5)
    sub = jnp.minimum(s, n_steps - 1) % tiles_per_seq
    prev_sub = jnp.maximum(s - 1, 0) % tiles_per_seq

    @pl.when(s == 0)
    def _():
        raw_odd_ref[...] = jnp.zeros_like(raw_odd_ref)

    @pl.when(prev_sub == 0)
    def _():
        carry_ref[...] = jnp.zeros_like(carry_ref)

    def step(raw_w, raw_r):
        for src_ref, dst_ref in zip(side_in, side_out):
            dst_ref[...] = src_ref[...].astype(BF16)
        packed_rows = pl.ds(pl.multiple_of(sub * (tm // 2), tm // 2), tm // 2)
        h = _unpack_rows(h_ref[packed_rows, :])
        g = _section_outputs(lambda sec: raw_r[sec], u_ref, gv_ref, ga_ref, gb_ref)
        gext_ref[:, 0:CARRY_ROWS, :] = carry_ref[...]
        for sl in range(n_slab):
            gext_ref[sl, CARRY_ROWS:CARRY_ROWS + tm, :] = g[:, sl * LANES:(sl + 1) * LANES]
            carry_ref[sl] = g[tm - CARRY_ROWS:, sl * LANES:(sl + 1) * LANES]
        tail_ref[0] = g[tm - CARRY_ROWS:, :]
        _causal_conv(gext_ref, cout_ref, wdw_ref, bdw_ref, c_ref)

        for sec in range(N_SEC):
            raw_w[sec] = _dot(h, w_refs[sec][...])

    @pl.when(s % 2 == 0)
    def _():
        step(raw_even_ref, raw_odd_ref)

    @pl.when(s % 2 == 1)
    def _():
        step(raw_odd_ref, raw_even_ref)


def _proj_sample_body(h_ref, w0, w1, w2, w3, w4, w5, wdw_ref, bdw_ref, cache_ref,
                      u_ref, gv_ref, c_ref, ga_ref, gb_ref, g_ref, *, n_seq):
    w_refs = (w0, w1, w2, w3, w4, w5)
    k_conv = wdw_ref.shape[0]
    n_prev = k_conv - 1
    t_new = 2 * h_ref.shape[0] // n_seq

    h = _unpack_rows(h_ref[...])
    g_ref[...] = _section_outputs(lambda sec: _dot(h, w_refs[sec][...]),
                                  u_ref, gv_ref, ga_ref, gb_ref)

    bias = bdw_ref[...]
    for t in range(t_new):
        acc = jnp.broadcast_to(bias, (n_seq, bias.shape[1]))
        for k in range(k_conv):
            m = t + k
            if m < n_prev:
                slab = cache_ref[m]
            else:
                slab = g_ref[(m - n_prev) * n_seq:(m - n_prev + 1) * n_seq, :]
            acc = acc + wdw_ref[k:k + 1, :] * slab
        c_ref[t * n_seq:(t + 1) * n_seq, :] = acc


def _proj_out_shapes(m, dsec):
    return [
        jax.ShapeDtypeStruct((m, dsec), BF16),
        jax.ShapeDtypeStruct((m, dsec), F32),
        jax.ShapeDtypeStruct((m, dsec), F32),
        jax.ShapeDtypeStruct((m, dsec), BF16),
        jax.ShapeDtypeStruct((m, dsec), BF16),
    ]


def _proj_prompt(h, w_in, w_dw, b_dw, *, tm, tn, seq_len, side_cast=()):
    m, d = 2 * h.shape[0], h.shape[1]
    dsec = w_in.shape[1] // N_SEC
    nblk = dsec // tn
    n_slab = tn // LANES
    k_conv = w_dw.shape[0]
    assert m % tm == 0 and dsec % tn == 0 and tn % LANES == 0
    assert seq_len % tm == 0 and tm >= CARRY_ROWS
    assert tm % (CONV_GROUPS * CONV_STRIDE * SUBLANES) == 0
    tps = seq_len // tm
    n_steps = (m // tm) * nblk
    seq_of = lambda t: t // (nblk * tps)
    col_of = lambda t: (t // tps) % nblk
    row_tile_of = lambda t: seq_of(t) * tps + t % tps
    cur = lambda s: jnp.minimum(s, n_steps - 1)
    prev = lambda s: jnp.maximum(s - 1, 0)
    w_specs = [pl.BlockSpec((d, tn), lambda s, sec=sec: (0, sec * nblk + col_of(cur(s))))
               for sec in range(N_SEC)]
    out_tile = lambda s: (row_tile_of(prev(s)), col_of(prev(s)))
    side_specs, side_shapes = [], []
    for w in side_cast:
        side_specs.append(_side_spec(w.shape, n_steps, lambda s: s))
        side_shapes.append(jax.ShapeDtypeStruct(w.shape, BF16))
    return pl.pallas_call(
        functools.partial(_proj_prompt_body, n_steps=n_steps, tiles_per_seq=tps,
                          n_side=len(side_cast)),
        grid=(n_steps + 1,),
        in_specs=[
            pl.BlockSpec((seq_len // 2, d), lambda s: (seq_of(cur(s)), 0)),
            *w_specs,
            pl.BlockSpec((k_conv, tn), lambda s: (0, col_of(prev(s)))),
            pl.BlockSpec((1, tn), lambda s: (0, col_of(prev(s)))),
            *side_specs,
        ],
        out_specs=[pl.BlockSpec((tm, tn), out_tile) for _ in range(5)]
        + [pl.BlockSpec((1, CARRY_ROWS, tn),
                        lambda s: (row_tile_of(prev(s)), 0, col_of(prev(s))))]
        + side_specs,
        out_shape=_proj_out_shapes(m, dsec)
        + [jax.ShapeDtypeStruct((m // tm, CARRY_ROWS, dsec), F32)] + side_shapes,
        scratch_shapes=[
            pltpu.VMEM((N_SEC, tm, tn), F32),
            pltpu.VMEM((N_SEC, tm, tn), F32),
            pltpu.VMEM((n_slab, CARRY_ROWS + tm, LANES), F32),
            pltpu.VMEM((n_slab, tm, LANES), F32),
            pltpu.VMEM((n_slab, CARRY_ROWS, LANES), F32),
        ],
        compiler_params=_params(1),
        name="proj_prompt",
    )(h, *([w_in] * N_SEC), w_dw, b_dw, *side_cast)


def _proj_sample(h, w_in, w_dw, b_dw, cache_t, *, tn):
    m, d = 2 * h.shape[0], h.shape[1]
    dsec = w_in.shape[1] // N_SEC
    nblk = dsec // tn
    k_conv = w_dw.shape[0]
    n_seq = cache_t.shape[1]
    assert dsec % tn == 0 and m % n_seq == 0
    w_specs = [pl.BlockSpec((d, tn), lambda j, sec=sec: (0, sec * nblk + j))
               for sec in range(N_SEC)]
    return pl.pallas_call(
        functools.partial(_proj_sample_body, n_seq=n_seq),
        grid=(nblk,),
        in_specs=[
            pl.BlockSpec((m // 2, d), lambda j: (0, 0)),
            *w_specs,
            pl.BlockSpec((k_conv, tn), lambda j: (0, j)),
            pl.BlockSpec((1, tn), lambda j: (0, j)),
            pl.BlockSpec((k_conv - 1, n_seq, tn), lambda j: (0, 0, j)),
        ],
        out_specs=[pl.BlockSpec((m, tn), lambda j: (0, j)) for _ in range(6)],
        out_shape=_proj_out_shapes(m, dsec) + [jax.ShapeDtypeStruct((m, dsec), F32)],
        compiler_params=_params(1),
        name="proj_sample",
    )(h, *([w_in] * N_SEC), w_dw, b_dw, cache_t)


def _mixa_prompt_body(u_ref, gv_ref, c_ref, ws_ref, bs_ref, lnvg_ref, lnvb_ref, lncg_ref, lncb_ref,
                      a_ref, cc_ref, vst_ref, vb_ref):
    tr = u_ref.shape[0]
    n_head, chunk, _ = ws_ref.shape
    hd = u_ref.shape[1] // n_head
    for r in range(0, tr, ROW_CHUNK):
        rows = slice(r, r + ROW_CHUNK)
        v = _layer_norm(gv_ref[rows, :], lnvg_ref[...], lnvb_ref[...])
        if r >= tr - chunk:
            vst_ref[0, r - (tr - chunk):r - (tr - chunk) + ROW_CHUNK, :] = v
        vb_ref[rows, :] = v.astype(BF16)
        cc_ref[r // 2:(r + ROW_CHUNK) // 2, :] = _pack_rows(jax.nn.silu(
            _layer_norm(c_ref[rows, :], lncg_ref[...], lncb_ref[...])))
    row = lax.broadcasted_iota(jnp.int32, (chunk, chunk), 0)
    col = lax.broadcasted_iota(jnp.int32, (chunk, chunk), 1)
    causal = (row >= col).astype(F32)
    for h in range(n_head):
        w = (ws_ref[h] * causal).astype(BF16)
        cols = slice(h * hd, (h + 1) * hd)
        for c in range(tr // chunk):
            rows = slice(c * chunk, (c + 1) * chunk)
            s = _dot(w, vb_ref[rows, cols]) + bs_ref[:, cols]
            packed_rows = slice(c * chunk // 2, (c + 1) * chunk // 2)
            a_ref[packed_rows, cols] = _pack_rows(u_ref[rows, cols].astype(F32) * s)


def _mixa_sample_body(u_ref, gv_ref, c_ref, coef_ref, bs_ref, lnvg_ref, lnvb_ref, lncg_ref,
                      lncb_ref, a_ref, cc_ref, v_ref, *, n_seq):
    t_new = u_ref.shape[0] // n_seq
    v_ref[...] = _layer_norm(gv_ref[...], lnvg_ref[...], lnvb_ref[...])
    for i in range(t_new):
        rows_i = slice(i * n_seq, (i + 1) * n_seq)
        s = jnp.broadcast_to(bs_ref[i:i + 1, :], (n_seq, bs_ref.shape[1]))
        for j in range(i + 1):
            s = s + coef_ref[i, j:j + 1, :] * v_ref[j * n_seq:(j + 1) * n_seq, :]
        a_ref[i * n_seq // 2:(i + 1) * n_seq // 2, :] = _pack_rows(
            u_ref[rows_i, :].astype(F32) * s)
    cc_ref[...] = _pack_rows(
        jax.nn.silu(_layer_norm(c_ref[...], lncg_ref[...], lncb_ref[...])))


def _mixa_prompt(u, gv, c, w_s, bs_slab, ln_v_g, ln_v_b, ln_c_g, ln_c_b, *, tr, seq_len):
    m, d = u.shape
    n_head, chunk, _ = w_s.shape
    assert seq_len % tr == 0 and tr % chunk == 0 and chunk % ROW_CHUNK == 0
    tiles_per_seq = seq_len // tr
    row = lambda i: (i, 0)
    const2 = lambda i: (0, 0)
    return pl.pallas_call(
        _mixa_prompt_body,
        grid=(m // tr,),
        in_specs=[
            pl.BlockSpec((tr, d), row),
            pl.BlockSpec((tr, d), row),
            pl.BlockSpec((tr, d), row),
            pl.BlockSpec((n_head, chunk, chunk), lambda i: (0, 0, 0)),
            pl.BlockSpec((chunk, d), const2),
            pl.BlockSpec((1, d), const2),
            pl.BlockSpec((1, d), const2),
            pl.BlockSpec((1, d), const2),
            pl.BlockSpec((1, d), const2),
        ],
        out_specs=[
            pl.BlockSpec((tr // 2, d), row),
            pl.BlockSpec((tr // 2, d), row),
            pl.BlockSpec((1, chunk, d), lambda i: (i // tiles_per_seq, 0, 0)),
        ],
        out_shape=[
            jax.ShapeDtypeStruct((m // 2, d), jnp.uint32),
            jax.ShapeDtypeStruct((m // 2, d), jnp.uint32),
            jax.ShapeDtypeStruct((m // seq_len, chunk, d), F32),
        ],
        scratch_shapes=[pltpu.VMEM((tr, d), BF16)],
        compiler_params=_params(1),
        name="mixa_prompt",
    )(u, gv, c, w_s, bs_slab, ln_v_g, ln_v_b, ln_c_g, ln_c_b)


def _mixa_sample(u, gv, c, coef, bs_rows, ln_v_g, ln_v_b, ln_c_g, ln_c_b, *, n_seq):
    m, d = u.shape
    t_new = m // n_seq
    whole = lambda i: (0, 0)
    return pl.pallas_call(
        functools.partial(_mixa_sample_body, n_seq=n_seq),
        grid=(1,),
        in_specs=[
            pl.BlockSpec((m, d), whole),
            pl.BlockSpec((m, d), whole),
            pl.BlockSpec((m, d), whole),
            pl.BlockSpec((t_new, t_new, d), lambda i: (0, 0, 0)),
            pl.BlockSpec((t_new, d), whole),
            pl.BlockSpec((1, d), whole),
            pl.BlockSpec((1, d), whole),
            pl.BlockSpec((1, d), whole),
            pl.BlockSpec((1, d), whole),
        ],
        out_specs=[pl.BlockSpec((m // 2, d), whole), pl.BlockSpec((m // 2, d), whole),
                   pl.BlockSpec((m, d), whole)],
        out_shape=[
            jax.ShapeDtypeStruct((m // 2, d), jnp.uint32),
            jax.ShapeDtypeStruct((m // 2, d), jnp.uint32),
            jax.ShapeDtypeStruct((m, d), F32),
        ],
        compiler_params=_params(1),
        name="mixa_sample",
    )(u, gv, c, coef, bs_rows, ln_v_g, ln_v_b, ln_c_g, ln_c_b)


def _mixb_body(x_ref, a_ref, cc_ref, ga_ref, gb_ref, wpa_ref, wpb_ref, wout_ref, o_ref):
    j = pl.program_id(1)

    @pl.when(j == 0)
    def _():
        o_ref[...] = jnp.zeros_like(o_ref)

    ya = _dot(_unpack_rows(a_ref[...]), wpa_ref[...])
    yb = _dot(_unpack_rows(cc_ref[...]), wpb_ref[...])
    merged = (ga_ref[...].astype(F32) * ya + gb_ref[...].astype(F32) * yb).astype(BF16)
    o_ref[...] += _dot(merged, wout_ref[...])

    @pl.when(j == pl.num_programs(1) - 1)
    def _():
        def add_chunk(rows, _):
            o_ref[rows, :] = x_ref[rows, :] + o_ref[rows, :]

        _for_each_row_chunk(x_ref.shape[0], add_chunk)


def _mixb(x, a, cc, ga, gb, w_pa, w_pb, w_out, *, tr, tj):
    m, d = x.shape
    dm = w_pa.shape[1]
    assert m % tr == 0 and dm % tj == 0 and tr % ROW_CHUNK == 0
    nj = dm // tj
    row = lambda i, j: (i, 0)
    x_row = lambda i, j: (jnp.where(j >= nj // 2, i, jnp.maximum(i - 1, 0)), 0)
    return pl.pallas_call(
        _mixb_body,
        grid=(m // tr, nj),
        in_specs=[
            pl.BlockSpec((tr, d), x_row),
            pl.BlockSpec((tr // 2, a.shape[1]), row),
            pl.BlockSpec((tr // 2, cc.shape[1]), row),
            pl.BlockSpec((tr, tj), lambda i, j: (i, j)),
            pl.BlockSpec((tr, tj), lambda i, j: (i, j)),
            pl.BlockSpec((w_pa.shape[0], tj), lambda i, j: (0, j)),
            pl.BlockSpec((w_pb.shape[0], tj), lambda i, j: (0, j)),
            pl.BlockSpec((tj, d), lambda i, j: (j, 0)),
        ],
        out_specs=pl.BlockSpec((tr, d), row),
        out_shape=jax.ShapeDtypeStruct((m, d), F32),
        compiler_params=_params(),
        name="mixb",
    )(x, a, cc, ga, gb, w_pa, w_pb, w_out)


def kernel(x_prompt, x_sample, cache_conv, ffn1_norm, ffn1_w_gu, ffn1_w_down, mix_norm, w_in, w_s, b_s, ln_v_g, ln_v_b, w_pa, w_dw, b_dw, ln_c_g, ln_c_b, w_pb, w_out, ffn2_norm, ffn2_w_gu, ffn2_w_down, final_norm):
    n_p, t_p, d = x_prompt.shape
    n_s, t_s, _ = x_sample.shape
    depth, n_head, chunk, _ = w_s.shape
    d_a = ln_v_g.shape[1]
    hd = d_a // n_head
    k_conv = w_dw.shape[1]
    assert t_s <= chunk and k_conv - 1 <= CARRY_ROWS and n_s % (2 * SUBLANES) == 0
    last_len = t_p - chunk * ((t_p - 1) // chunk)
    assert last_len == chunk, "prompt length must be a whole number of chunks"

    xp = x_prompt.reshape(n_p * t_p, d)
    xs = jnp.swapaxes(x_sample, 0, 1).reshape(t_s * n_s, d)
    m_s = t_s * n_s
    row2 = lambda v: v.reshape(1, -1)
    fgain = row2(final_norm)

    tiles = _tile_sizes()
    tiles_per_seq = t_p // tiles.proj_rows
    vp_list, vs_list, cp_list, cs_list = [], [], [], []
    for l in range(depth):
        last = l == depth - 1
        w1gu, w1d = ffn1_w_gu[l].astype(BF16), ffn1_w_down[l].astype(BF16)
        g1, gm, g2 = row2(ffn1_norm[l]), row2(mix_norm[l]), row2(ffn2_norm[l])
        bdw = row2(b_dw[l])
        lnvg, lnvb, lncg, lncb = row2(ln_v_g[l]), row2(ln_v_b[l]), row2(ln_c_g[l]), row2(ln_c_b[l])
        bs_slab = jnp.repeat(b_s[l].T, hd, axis=1)
        coef = jnp.repeat(jnp.transpose(w_s[l][:, :t_s, :t_s], (1, 2, 0)), hd, axis=2)
        cache_t = jnp.swapaxes(cache_conv[l], 0, 1)

        xp, hp, win, w2gu = _ffn(xp, g1, w1gu, w1d, gm, tm=tiles.ffn_rows,
                                 tf=tiles.ffn_hidden_emit, emit_next_norm=True,
                                 side_cast=(w_in[l], ffn2_w_gu[l]))
        u, gv, c, ga, gb, tail, w2d, wpa, wpb, wout = _proj_prompt(
            hp, win, w_dw[l], bdw, tm=tiles.proj_rows, tn=tiles.proj_cols, seq_len=t_p,
            side_cast=(ffn2_w_down[l], w_pa[l], w_pb[l], w_out[l]))
        tail = tail[tiles_per_seq - 1::tiles_per_seq]
        a, cc, v_p = _mixa_prompt(u, gv, c, w_s[l], bs_slab, lnvg, lnvb, lncg, lncb,
                                  tr=tiles.mixa_rows, seq_len=t_p)
        xp = _mixb(xp, a, cc, ga, gb, wpa, wpb, wout, tr=tiles.mix_rows, tj=tiles.mix_cols)
        (xp,) = _ffn(xp, g2, w2gu, w2d, fgain, tm=tiles.ffn_rows, tf=tiles.ffn_hidden,
                     final_norm=last)
        vp_list.append(v_p)
        cp_list.append(tail[:, CARRY_ROWS - (k_conv - 1):, :])

        xs, hs = _ffn(xs, g1, w1gu, w1d, gm, tm=m_s, tf=tiles.ffn_hidden, emit_next_norm=True)
        u, gv, c, ga, gb, g_s = _proj_sample(hs, win, w_dw[l], bdw, cache_t, tn=tiles.proj_cols)
        a, cc, v_s = _mixa_sample(u, gv, c, coef, bs_slab[:t_s], lnvg, lnvb, lncg, lncb, n_seq=n_s)
        xs = _mixb(xs, a, cc, ga, gb, wpa, wpb, wout, tr=m_s, tj=tiles.mix_cols)
        (xs,) = _ffn(xs, g2, w2gu, w2d, fgain, tm=m_s, tf=tiles.ffn_hidden, final_norm=last)
        vs_list.append(jnp.swapaxes(v_s.reshape(t_s, n_s, d_a), 0, 1))
        g_seq = jnp.swapaxes(g_s.reshape(t_s, n_s, -1), 0, 1)
        cs_list.append(jnp.concatenate([cache_conv[l], g_seq], axis=1)[:, -(k_conv - 1):])

    y_prompt = xp.reshape(n_p, t_p, d)
    y_sample = jnp.swapaxes(xs.reshape(t_s, n_s, d), 0, 1)
    return (y_prompt, y_sample, jnp.stack(vp_list), jnp.stack(vs_list),
            jnp.stack(cp_list), jnp.stack(cs_list))
```

```python
import functools
from typing import NamedTuple

import jax
import jax.numpy as jnp
from jax import lax
from jax.experimental import pallas as pl
from jax.experimental.pallas import tpu as pltpu

F32 = jnp.float32
BF16 = jnp.bfloat16
EPS = 1e-6

V7X_VMEM_BYTES = 64 * 1024 * 1024
VMEM_LIMIT_BYTES = V7X_VMEM_BYTES - 4 * 1024 * 1024
SUBLANES = 8
LANES = 128
CONV_STRIDE = 4
CONV_GROUPS = 2
CARRY_ROWS = 32
ROW_CHUNK = 64

SEC_UA, SEC_VA, SEC_GLU_A, SEC_GLU_B, SEC_GATE_A, SEC_GATE_B = range(6)
N_SEC = 6


class _Tiles(NamedTuple):
    ffn_rows: int
    ffn_hidden: int
    ffn_hidden_emit: int
    proj_rows: int
    proj_cols: int
    mixa_rows: int
    mix_rows: int
    mix_cols: int


def _tile_sizes():
    return _Tiles(ffn_rows=1024, ffn_hidden=512, ffn_hidden_emit=256, proj_rows=512, proj_cols=256,
                  mixa_rows=512, mix_rows=1024, mix_cols=256)


def _params(n_axes=2):
    return pltpu.CompilerParams(
        dimension_semantics=("arbitrary",) * n_axes,
        vmem_limit_bytes=VMEM_LIMIT_BYTES,
    )


def _rms(x, gain):
    ms = jnp.mean(x * x, axis=-1, keepdims=True)
    return x * lax.rsqrt(ms + EPS) * gain


def _layer_norm(x, gain, bias):
    mu = jnp.mean(x, axis=-1, keepdims=True)
    xc = x - mu
    var = jnp.mean(xc * xc, axis=-1, keepdims=True)
    return xc * lax.rsqrt(var + EPS) * gain + bias


def _dot(a, b):
    return jnp.dot(a, b, preferred_element_type=F32)


def _pack_rows(x):
    return pltpu.bitcast(x.astype(BF16), jnp.uint32)


def _unpack_rows(packed):
    return pltpu.bitcast(packed, BF16)


def _for_each_row_chunk(n_rows, fn):
    half = ROW_CHUNK // 2

    def body(c, carry):
        fn(pl.ds(pl.multiple_of(c * ROW_CHUNK, ROW_CHUNK), ROW_CHUNK),
           pl.ds(pl.multiple_of(c * half, half), half))
        return carry

    lax.fori_loop(0, n_rows // ROW_CHUNK, body, 0, unroll=2)


def _ffn_body(x_ref, gain_ref, wg_ref, wu_ref, wd_ref, post_gain_ref, *rest,
              final_norm, emit_next_norm, n_side):
    side_in = rest[:n_side]
    o_ref = rest[n_side]
    rest = rest[n_side + 1:]
    if emit_next_norm:
        next_ref, rest = rest[0], rest[1:]
    side_out, (xn_ref,) = rest[:n_side], rest[n_side:]
    k = pl.program_id(1)
    tm = x_ref.shape[0]

    @pl.when(k == 0)
    def _():
        def norm_chunk(rows, _):
            xn_ref[rows, :] = _rms(x_ref[rows, :], gain_ref[...]).astype(BF16)
            o_ref[rows, :] = jnp.zeros((ROW_CHUNK, o_ref.shape[1]), F32)

        _for_each_row_chunk(tm, norm_chunk)

    xn = xn_ref[...]
    gate = _dot(xn, wg_ref[...])
    up = _dot(xn, wu_ref[...])
    act = (jax.nn.silu(gate) * up).astype(BF16)
    for src_ref, dst_ref in zip(side_in, side_out):
        dst_ref[...] = src_ref[...].astype(BF16)
    o_ref[...] += _dot(act, wd_ref[...])

    @pl.when(k == pl.num_programs(1) - 1)
    def _():
        def finish_chunk(rows, packed_rows):
            y = x_ref[rows, :] + 0.5 * o_ref[rows, :]
            if final_norm:
                y = _rms(y, post_gain_ref[...])
            o_ref[rows, :] = y
            if emit_next_norm:
                next_ref[packed_rows, :] = _pack_rows(_rms(y, post_gain_ref[...]))

        _for_each_row_chunk(tm, finish_chunk)


def _side_block(shape, max_blocks):
    r, c = shape
    for n_blocks in range(max_blocks, 0, -1):
        for n_col_blocks in range(1, n_blocks + 1):
            if n_blocks % n_col_blocks or c % n_col_blocks or r % (n_blocks // n_col_blocks):
                continue
            br, bc = r // (n_blocks // n_col_blocks), c // n_col_blocks
            if br % (2 * SUBLANES) == 0 and bc % LANES == 0:
                return br, bc
    raise ValueError(f"cannot cut {shape} into aligned blocks")


def _side_spec(shape, max_blocks, step_of):
    br, bc = _side_block(shape, max_blocks)
    n_col_blocks = shape[1] // bc
    n_blocks = (shape[0] // br) * n_col_blocks

    def index_map(*grid_idx):
        t = jnp.minimum(step_of(*grid_idx), n_blocks - 1)
        return t // n_col_blocks, t % n_col_blocks

    return pl.BlockSpec((br, bc), index_map)


def _ffn(x, gain, w_gu, w_down, post_gain, *, tm, tf, final_norm=False, emit_next_norm=False,
         side_cast=()):
    m, d = x.shape
    f = w_down.shape[0]
    nf = f // tf
    n_steps = (m // tm) * nf
    assert m % tm == 0 and f % tf == 0 and tm % ROW_CHUNK == 0
    assert not (final_norm and emit_next_norm)
    rows = lambda i, k: (i, 0)
    out_specs = [pl.BlockSpec((tm, d), rows)]
    out_shape = [jax.ShapeDtypeStruct((m, d), F32)]
    if emit_next_norm:
        out_specs.append(pl.BlockSpec((tm // 2, d), rows))
        out_shape.append(jax.ShapeDtypeStruct((m // 2, d), jnp.uint32))
    side_specs = []
    for w in side_cast:
        spec = _side_spec(w.shape, n_steps, lambda i, k: i * nf + k)
        side_specs.append(spec)
        out_specs.append(spec)
        out_shape.append(jax.ShapeDtypeStruct(w.shape, BF16))
    return pl.pallas_call(
        functools.partial(_ffn_body, final_norm=final_norm, emit_next_norm=emit_next_norm,
                          n_side=len(side_cast)),
        grid=(m // tm, nf),
        in_specs=[
            pl.BlockSpec((tm, d), rows),
            pl.BlockSpec((1, d), lambda i, k: (0, 0)),
            pl.BlockSpec((d, tf), lambda i, k: (0, k)),
            pl.BlockSpec((d, tf), lambda i, k: (0, k + nf)),
            pl.BlockSpec((tf, d), lambda i, k: (k, 0)),
            pl.BlockSpec((1, d), lambda i, k: (0, 0)),
            *side_specs,
        ],
        out_specs=out_specs,
        out_shape=out_shape,
        scratch_shapes=[pltpu.VMEM((tm, d), BF16)],
        compiler_params=_params(),
        name="ffn",
    )(x, gain, w_gu, w_gu, w_down, post_gain, *side_cast)


def _section_outputs(pre, u_ref, gv_ref, ga_ref, gb_ref):
    u_ref[...] = jax.nn.gelu(pre(SEC_UA)).astype(BF16)
    gv_ref[...] = jax.nn.gelu(pre(SEC_VA))
    ga_ref[...] = jax.nn.sigmoid(pre(SEC_GATE_A)).astype(BF16)
    gb_ref[...] = jax.nn.sigmoid(pre(SEC_GATE_B)).astype(BF16)
    return pre(SEC_GLU_A) * jax.nn.sigmoid(pre(SEC_GLU_B))


def _causal_conv(gext_ref, cout_ref, wdw_ref, bdw_ref, c_ref):
    n_slab, tm, _ = cout_ref.shape
    k_conv = wdw_ref.shape[0]
    shift = CARRY_ROWS - (k_conv - 1)
    group = CONV_STRIDE * SUBLANES
    starts = [b * group + p for b in range(CONV_GROUPS) for p in range(CONV_STRIDE)]
    for s in range(n_slab):
        lanes = slice(s * LANES, (s + 1) * LANES)
        bias = jnp.broadcast_to(bdw_ref[:, lanes], (SUBLANES, LANES))
        for r0 in range(0, tm, CONV_GROUPS * group):
            accs = [bias for _ in starts]
            for k in range(k_conv):
                wk = jnp.broadcast_to(wdw_ref[k:k + 1, lanes], (SUBLANES, LANES))
                for q, st in enumerate(starts):
                    rows = pl.ds(r0 + st + shift + k, SUBLANES, stride=CONV_STRIDE)
                    accs[q] = accs[q] + wk * gext_ref[s, rows, :]
            for q, st in enumerate(starts):
                cout_ref[s, pl.ds(r0 + st, SUBLANES, stride=CONV_STRIDE), :] = accs[q]
        c_ref[:, lanes] = cout_ref[s]


def _proj_prompt_body(h_ref, w0, w1, w2, w3, w4, w5, wdw_ref, bdw_ref, *rest,
                      n_steps, tiles_per_seq, n_side):
    side_in, rest = rest[:n_side], rest[n_side:]
    u_ref, gv_ref, c_ref, ga_ref, gb_ref, tail_ref = rest[:6]
    side_out, rest = rest[6:6 + n_side], rest[6 + n_side:]
    raw_even_ref, raw_odd_ref, gext_ref, cout_ref, carry_ref = rest
    s = pl.program_id(0)
    n_slab, _, _ = cout_ref.shape
    tm = cout_ref.shape[1]
    w_refs = (w0, w1, w2, w3, w4, w5)
    sub = jnp.minimum(s, n_steps - 1) % tiles_per_seq
    prev_sub = jnp.maximum(s - 1, 0) % tiles_per_seq

    @pl.when(s == 0)
    def _():
        raw_odd_ref[...] = jnp.zeros_like(raw_odd_ref)

    @pl.when(prev_sub == 0)
    def _():
        carry_ref[...] = jnp.zeros_like(carry_ref)

    def step(raw_w, raw_r):
        for src_ref, dst_ref in zip(side_in, side_out):
            dst_ref[...] = src_ref[...].astype(BF16)
        packed_rows = pl.ds(pl.multiple_of(sub * (tm // 2), tm // 2), tm // 2)
        h = _unpack_rows(h_ref[packed_rows, :])
        g = _section_outputs(lambda sec: raw_r[sec], u_ref, gv_ref, ga_ref, gb_ref)
        gext_ref[:, 0:CARRY_ROWS, :] = carry_ref[...]
        for sl in range(n_slab):
            gext_ref[sl, CARRY_ROWS:CARRY_ROWS + tm, :] = g[:, sl * LANES:(sl + 1) * LANES]
            carry_ref[sl] = g[tm - CARRY_ROWS:, sl * LANES:(sl + 1) * LANES]
        tail_ref[0] = g[tm - CARRY_ROWS:, :]
        _causal_conv(gext_ref, cout_ref, wdw_ref, bdw_ref, c_ref)

        for sec in range(N_SEC):
            raw_w[sec] = _dot(h, w_refs[sec][...])

    @pl.when(s % 2 == 0)
    def _():
        step(raw_even_ref, raw_odd_ref)

    @pl.when(s % 2 == 1)
    def _():
        step(raw_odd_ref, raw_even_ref)


def _proj_sample_body(h_ref, w0, w1, w2, w3, w4, w5, wdw_ref, bdw_ref, cache_ref,
                      u_ref, gv_ref, c_ref, ga_ref, gb_ref, g_ref, *, n_seq):
    w_refs = (w0, w1, w2, w3, w4, w5)
    k_conv = wdw_ref.shape[0]
    n_prev = k_conv - 1
    t_new = 2 * h_ref.shape[0] // n_seq

    h = _unpack_rows(h_ref[...])
    g_ref[...] = _section_outputs(lambda sec: _dot(h, w_refs[sec][...]),
                                  u_ref, gv_ref, ga_ref, gb_ref)

    bias = bdw_ref[...]
    for t in range(t_new):
        acc = jnp.broadcast_to(bias, (n_seq, bias.shape[1]))
        for k in range(k_conv):
            m = t + k
            if m < n_prev:
                slab = cache_ref[m]
            else:
                slab = g_ref[(m - n_prev) * n_seq:(m - n_prev + 1) * n_seq, :]
            acc = acc + wdw_ref[k:k + 1, :] * slab
        c_ref[t * n_seq:(t + 1) * n_seq, :] = acc


def _proj_out_shapes(m, dsec):
    return [
        jax.ShapeDtypeStruct((m, dsec), BF16),
        jax.ShapeDtypeStruct((m, dsec), F32),
        jax.ShapeDtypeStruct((m, dsec), F32),
        jax.ShapeDtypeStruct((m, dsec), BF16),
        jax.ShapeDtypeStruct((m, dsec), BF16),
    ]


def _proj_prompt(h, w_in, w_dw, b_dw, *, tm, tn, seq_len, side_cast=()):
    m, d = 2 * h.shape[0], h.shape[1]
    dsec = w_in.shape[1] // N_SEC
    nblk = dsec // tn
    n_slab = tn // LANES
    k_conv = w_dw.shape[0]
    assert m % tm == 0 and dsec % tn == 0 and tn % LANES == 0
    assert seq_len % tm == 0 and tm >= CARRY_ROWS
    assert tm % (CONV_GROUPS * CONV_STRIDE * SUBLANES) == 0
    tps = seq_len // tm
    n_steps = (m // tm) * nblk
    seq_of = lambda t: t // (nblk * tps)
    col_of = lambda t: (t // tps) % nblk
    row_tile_of = lambda t: seq_of(t) * tps + t % tps
    cur = lambda s: jnp.minimum(s, n_steps - 1)
    prev = lambda s: jnp.maximum(s - 1, 0)
    w_specs = [pl.BlockSpec((d, tn), lambda s, sec=sec: (0, sec * nblk + col_of(cur(s))))
               for sec in range(N_SEC)]
    out_tile = lambda s: (row_tile_of(prev(s)), col_of(prev(s)))
    side_specs, side_shapes = [], []
    for w in side_cast:
        side_specs.append(_side_spec(w.shape, n_steps, lambda s: s))
        side_shapes.append(jax.ShapeDtypeStruct(w.shape, BF16))
    return pl.pallas_call(
        functools.partial(_proj_prompt_body, n_steps=n_steps, tiles_per_seq=tps,
                          n_side=len(side_cast)),
        grid=(n_steps + 1,),
        in_specs=[
            pl.BlockSpec((seq_len // 2, d), lambda s: (seq_of(cur(s)), 0)),
            *w_specs,
            pl.BlockSpec((k_conv, tn), lambda s: (0, col_of(prev(s)))),
            pl.BlockSpec((1, tn), lambda s: (0, col_of(prev(s)))),
            *side_specs,
        ],
        out_specs=[pl.BlockSpec((tm, tn), out_tile) for _ in range(5)]
        + [pl.BlockSpec((1, CARRY_ROWS, tn),
                        lambda s: (row_tile_of(prev(s)), 0, col_of(prev(s))))]
        + side_specs,
        out_shape=_proj_out_shapes(m, dsec)
        + [jax.ShapeDtypeStruct((m // tm, CARRY_ROWS, dsec), F32)] + side_shapes,
        scratch_shapes=[
            pltpu.VMEM((N_SEC, tm, tn), F32),
            pltpu.VMEM((N_SEC, tm, tn), F32),
            pltpu.VMEM((n_slab, CARRY_ROWS + tm, LANES), F32),
            pltpu.VMEM((n_slab, tm, LANES), F32),
            pltpu.VMEM((n_slab, CARRY_ROWS, LANES), F32),
        ],
        compiler_params=_params(1),
        name="proj_prompt",
    )(h, *([w_in] * N_SEC), w_dw, b_dw, *side_cast)


def _proj_sample(h, w_in, w_dw, b_dw, cache_t, *, tn):
    m, d = 2 * h.shape[0], h.shape[1]
    dsec = w_in.shape[1] // N_SEC
    nblk = dsec // tn
    k_conv = w_dw.shape[0]
    n_seq = cache_t.shape[1]
    assert dsec % tn == 0 and m % n_seq == 0
    w_specs = [pl.BlockSpec((d, tn), lambda j, sec=sec: (0, sec * nblk + j))
               for sec in range(N_SEC)]
    return pl.pallas_call(
        functools.partial(_proj_sample_body, n_seq=n_seq),
        grid=(nblk,),
        in_specs=[
            pl.BlockSpec((m // 2, d), lambda j: (0, 0)),
            *w_specs,
            pl.BlockSpec((k_conv, tn), lambda j: (0, j)),
            pl.BlockSpec((1, tn), lambda j: (0, j)),
            pl.BlockSpec((k_conv - 1, n_seq, tn), lambda j: (0, 0, j)),
        ],
        out_specs=[pl.BlockSpec((m, tn), lambda j: (0, j)) for _ in range(6)],
        out_shape=_proj_out_shapes(m, dsec) + [jax.ShapeDtypeStruct((m, dsec), F32)],
        compiler_params=_params(1),
        name="proj_sample",
    )(h, *([w_in] * N_SEC), w_dw, b_dw, cache_t)


def _mixa_prompt_body(u_ref, gv_ref, c_ref, ws_ref, bs_ref, lnvg_ref, lnvb_ref, lncg_ref, lncb_ref,
                      a_ref, cc_ref, vst_ref, vb_ref):
    tr = u_ref.shape[0]
    n_head, chunk, _ = ws_ref.shape
    hd = u_ref.shape[1] // n_head
    for r in range(0, tr, ROW_CHUNK):
        rows = slice(r, r + ROW_CHUNK)
        v = _layer_norm(gv_ref[rows, :], lnvg_ref[...], lnvb_ref[...])
        if r >= tr - chunk:
            vst_ref[0, r - (tr - chunk):r - (tr - chunk) + ROW_CHUNK, :] = v
        vb_ref[rows, :] = v.astype(BF16)
        cc_ref[r // 2:(r + ROW_CHUNK) // 2, :] = _pack_rows(jax.nn.silu(
            _layer_norm(c_ref[rows, :], lncg_ref[...], lncb_ref[...])))
    row = lax.broadcasted_iota(jnp.int32, (chunk, chunk), 0)
    col = lax.broadcasted_iota(jnp.int32, (chunk, chunk), 1)
    causal = (row >= col).astype(F32)
    for h in range(n_head):
        w = (ws_ref[h] * causal).astype(BF16)
        cols = slice(h * hd, (h + 1) * hd)
        for c in range(tr // chunk):
            rows = slice(c * chunk, (c + 1) * chunk)
            s = _dot(w, vb_ref[rows, cols]) + bs_ref[:, cols]
            packed_rows = slice(c * chunk // 2, (c + 1) * chunk // 2)
            a_ref[packed_rows, cols] = _pack_rows(u_ref[rows, cols].astype(F32) * s)


def _mixa_sample_body(u_ref, gv_ref, c_ref, coef_ref, bs_ref, lnvg_ref, lnvb_ref, lncg_ref,
                      lncb_ref, a_ref, cc_ref, v_ref, *, n_seq):
    t_new = u_ref.shape[0] // n_seq
    v_ref[...] = _layer_norm(gv_ref[...], lnvg_ref[...], lnvb_ref[...])
    for i in range(t_new):
        rows_i = slice(i * n_seq, (i + 1) * n_seq)
        s = jnp.broadcast_to(bs_ref[i:i + 1, :], (n_seq, bs_ref.shape[1]))
        for j in range(i + 1):
            s = s + coef_ref[i, j:j + 1, :] * v_ref[j * n_seq:(j + 1) * n_seq, :]
        a_ref[i * n_seq // 2:(i + 1) * n_seq // 2, :] = _pack_rows(
            u_ref[rows_i, :].astype(F32) * s)
    cc_ref[...] = _pack_rows(
        jax.nn.silu(_layer_norm(c_ref[...], lncg_ref[...], lncb_ref[...])))


def _mixa_prompt(u, gv, c, w_s, bs_slab, ln_v_g, ln_v_b, ln_c_g, ln_c_b, *, tr, seq_len):
    m, d = u.shape
    n_head, chunk, _ = w_s.shape
    assert seq_len % tr == 0 and tr % chunk == 0 and chunk % ROW_CHUNK == 0
    tiles_per_seq = seq_len // tr
    row = lambda i: (i, 0)
    const2 = lambda i: (0, 0)
    return pl.pallas_call(
        _mixa_prompt_body,
        grid=(m // tr,),
        in_specs=[
            pl.BlockSpec((tr, d), row),
            pl.BlockSpec((tr, d), row),
            pl.BlockSpec((tr, d), row),
            pl.BlockSpec((n_head, chunk, chunk), lambda i: (0, 0, 0)),
            pl.BlockSpec((chunk, d), const2),
            pl.BlockSpec((1, d), const2),
            pl.BlockSpec((1, d), const2),
            pl.BlockSpec((1, d), const2),
            pl.BlockSpec((1, d), const2),
        ],
        out_specs=[
            pl.BlockSpec((tr // 2, d), row),
            pl.BlockSpec((tr // 2, d), row),
            pl.BlockSpec((1, chunk, d), lambda i: (i // tiles_per_seq, 0, 0)),
        ],
        out_shape=[
            jax.ShapeDtypeStruct((m // 2, d), jnp.uint32),
            jax.ShapeDtypeStruct((m // 2, d), jnp.uint32),
            jax.ShapeDtypeStruct((m // seq_len, chunk, d), F32),
        ],
        scratch_shapes=[pltpu.VMEM((tr, d), BF16)],
        compiler_params=_params(1),
        name="mixa_prompt",
    )(u, gv, c, w_s, bs_slab, ln_v_g, ln_v_b, ln_c_g, ln_c_b)


def _mixa_sample(u, gv, c, coef, bs_rows, ln_v_g, ln_v_b, ln_c_g, ln_c_b, *, n_seq):
    m, d = u.shape
    t_new = m // n_seq
    whole = lambda i: (0, 0)
    return pl.pallas_call(
        functools.partial(_mixa_sample_body, n_seq=n_seq),
        grid=(1,),
        in_specs=[
            pl.BlockSpec((m, d), whole),
            pl.BlockSpec((m, d), whole),
            pl.BlockSpec((m, d), whole),
            pl.BlockSpec((t_new, t_new, d), lambda i: (0, 0, 0)),
            pl.BlockSpec((t_new, d), whole),
            pl.BlockSpec((1, d), whole),
            pl.BlockSpec((1, d), whole),
            pl.BlockSpec((1, d), whole),
            pl.BlockSpec((1, d), whole),
        ],
        out_specs=[pl.BlockSpec((m // 2, d), whole), pl.BlockSpec((m // 2, d), whole),
                   pl.BlockSpec((m, d), whole)],
        out_shape=[
            jax.ShapeDtypeStruct((m // 2, d), jnp.uint32),
            jax.ShapeDtypeStruct((m // 2, d), jnp.uint32),
            jax.ShapeDtypeStruct((m, d), F32),
        ],
        compiler_params=_params(1),
        name="mixa_sample",
    )(u, gv, c, coef, bs_rows, ln_v_g, ln_v_b, ln_c_g, ln_c_b)


def _mixb_body(x_ref, a_ref, cc_ref, ga_ref, gb_ref, wpa_ref, wpb_ref, wout_ref, o_ref):
    j = pl.program_id(1)

    @pl.when(j == 0)
    def _():
        o_ref[...] = jnp.zeros_like(o_ref)

    ya = _dot(_unpack_rows(a_ref[...]), wpa_ref[...])
    yb = _dot(_unpack_rows(cc_ref[...]), wpb_ref[...])
    merged = (ga_ref[...].astype(F32) * ya + gb_ref[...].astype(F32) * yb).astype(BF16)
    o_ref[...] += _dot(merged, wout_ref[...])

    @pl.when(j == pl.num_programs(1) - 1)
    def _():
        def add_chunk(rows, _):
            o_ref[rows, :] = x_ref[rows, :] + o_ref[rows, :]

        _for_each_row_chunk(x_ref.shape[0], add_chunk)


def _mixb(x, a, cc, ga, gb, w_pa, w_pb, w_out, *, tr, tj):
    m, d = x.shape
    dm = w_pa.shape[1]
    assert m % tr == 0 and dm % tj == 0 and tr % ROW_CHUNK == 0
    nj = dm // tj
    row = lambda i, j: (i, 0)
    x_row = lambda i, j: (jnp.where(j >= nj // 2, i, jnp.maximum(i - 1, 0)), 0)
    return pl.pallas_call(
        _mixb_body,
        grid=(m // tr, nj),
        in_specs=[
            pl.BlockSpec((tr, d), x_row),
            pl.BlockSpec((tr // 2, a.shape[1]), row),
            pl.BlockSpec((tr // 2, cc.shape[1]), row),
            pl.BlockSpec((tr, tj), lambda i, j: (i, j)),
            pl.BlockSpec((tr, tj), lambda i, j: (i, j)),
            pl.BlockSpec((w_pa.shape[0], tj), lambda i, j: (0, j)),
            pl.BlockSpec((w_pb.shape[0], tj), lambda i, j: (0, j)),
            pl.BlockSpec((tj, d), lambda i, j: (j, 0)),
        ],
        out_specs=pl.BlockSpec((tr, d), row),
        out_shape=jax.ShapeDtypeStruct((m, d), F32),
        compiler_params=_params(),
        name="mixb",
    )(x, a, cc, ga, gb, w_pa, w_pb, w_out)


def kernel(x_prompt, x_sample, cache_conv, ffn1_norm, ffn1_w_gu, ffn1_w_down, mix_norm, w_in, w_s, b_s, ln_v_g, ln_v_b, w_pa, w_dw, b_dw, ln_c_g, ln_c_b, w_pb, w_out, ffn2_norm, ffn2_w_gu, ffn2_w_down, final_norm):
    n_p, t_p, d = x_prompt.shape
    n_s, t_s, _ = x_sample.shape
    depth, n_head, chunk, _ = w_s.shape
    d_a = ln_v_g.shape[1]
    hd = d_a // n_head
    k_conv = w_dw.shape[1]
    assert t_s <= chunk and k_conv - 1 <= CARRY_ROWS and n_s % (2 * SUBLANES) == 0
    last_len = t_p - chunk * ((t_p - 1) // chunk)
    assert last_len == chunk, "prompt length must be a whole number of chunks"

    xp = x_prompt.reshape(n_p * t_p, d)
    xs = jnp.swapaxes(x_sample, 0, 1).reshape(t_s * n_s, d)
    m_s = t_s * n_s
    row2 = lambda v: v.reshape(1, -1)
    fgain = row2(final_norm)

    tiles = _tile_sizes()
    tiles_per_seq = t_p // tiles.proj_rows
    vp_list, vs_list, cp_list, cs_list = [], [], [], []
    for l in range(depth):
        last = l == depth - 1
        w1gu, w1d = ffn1_w_gu[l].astype(BF16), ffn1_w_down[l].astype(BF16)
        g1, gm, g2 = row2(ffn1_norm[l]), row2(mix_norm[l]), row2(ffn2_norm[l])
        bdw = row2(b_dw[l])
        lnvg, lnvb, lncg, lncb = row2(ln_v_g[l]), row2(ln_v_b[l]), row2(ln_c_g[l]), row2(ln_c_b[l])
        bs_slab = jnp.repeat(b_s[l].T, hd, axis=1)
        coef = jnp.repeat(jnp.transpose(w_s[l][:, :t_s, :t_s], (1, 2, 0)), hd, axis=2)
        cache_t = jnp.swapaxes(cache_conv[l], 0, 1)

        xp, hp, win, w2gu = _ffn(xp, g1, w1gu, w1d, gm, tm=tiles.ffn_rows,
                                 tf=tiles.ffn_hidden_emit, emit_next_norm=True,
                                 side_cast=(w_in[l], ffn2_w_gu[l]))
        u, gv, c, ga, gb, tail, w2d, wpa, wpb, wout = _proj_prompt(
            hp, win, w_dw[l], bdw, tm=tiles.proj_rows, tn=tiles.proj_cols, seq_len=t_p,
            side_cast=(ffn2_w_down[l], w_pa[l], w_pb[l], w_out[l]))
        tail = tail[tiles_per_seq - 1::tiles_per_seq]
        a, cc, v_p = _mixa_prompt(u, gv, c, w_s[l], bs_slab, lnvg, lnvb, lncg, lncb,
                                  tr=tiles.mixa_rows, seq_len=t_p)
        xp = _mixb(xp, a, cc, ga, gb, wpa, wpb, wout, tr=tiles.mix_rows, tj=tiles.mix_cols)
        (xp,) = _ffn(xp, g2, w2gu, w2d, fgain, tm=tiles.ffn_rows, tf=tiles.ffn_hidden,
                     final_norm=last)
        vp_list.append(v_p)
        cp_list.append(tail[:, CARRY_ROWS - (k_conv - 1):, :])

        xs, hs = _ffn(xs, g1, w1gu, w1d, gm, tm=m_s, tf=tiles.ffn_hidden, emit_next_norm=True)
        u, gv, c, ga, gb, g_s = _proj_sample(hs, win, w_dw[l], bdw, cache_t, tn=tiles.proj_cols)
        a, cc, v_s = _mixa_sample(u, gv, c, coef, bs_slab[:t_s], lnvg, lnvb, lncg, lncb, n_seq=n_s)
        xs = _mixb(xs, a, cc, ga, gb, wpa, wpb, wout, tr=m_s, tj=tiles.mix_cols)
        (xs,) = _ffn(xs, g2, w2gu, w2d, fgain, tm=m_s, tf=tiles.ffn_hidden, final_norm=last)
        vs_list.append(jnp.swapaxes(v_s.reshape(t_s, n_s, d_a), 0, 1))
        g_seq = jnp.swapaxes(g_s.reshape(t_s, n_s, -1), 0, 1)
        cs_list.append(jnp.concatenate([cache_conv[l], g_seq], axis=1)[:, -(k_conv - 1):])

    y_prompt = xp.reshape(n_p, t_p, d)
    y_sample = jnp.swapaxes(xs.reshape(t_s, n_s, d), 0, 1)
    return (y_prompt, y_sample, jnp.stack(vp_list), jnp.stack(vs_list),
            jnp.stack(cp_list), jnp.stack(cs_list))
```
